```python
import jax, jax.numpy as jnp
from jax import lax
import numpy as np

D_MODEL = 2048
BATCH = 2
SEQ = 8192
DEPTH = 1

EPS = 1e-6
M_HEADS = 4
M_HEAD_DIM = D_MODEL // 2 // M_HEADS
M_WIDTH = M_HEADS * M_HEAD_DIM
M_CHUNK = 64
CONV_W = 4
A_GROUPS = 4
A_HPG = 4
A_HEADS = A_GROUPS * A_HPG
A_HEAD_DIM = 64
A_WIDTH = A_HEADS * A_HEAD_DIM
CMP_BLOCK = 32
CMP_STRIDE = 16
CMP_HIDDEN = 256
SEL_BLOCK = 64
SEL_TOP = 16
WINDOW = 512
Q_BLOCK = 128
FORCE_SCORE = 1e4
P_HEADS = 8
P_NKEYS = 128
P_EXPERTS = P_NKEYS * P_NKEYS
P_TOPK = 16
P_QDIM = 256
P_HALF = P_QDIM // 2
P_TOK_BLOCK = 128
MV_OFF = 2 * M_WIDTH
MO_OFF = 3 * M_WIDTH
MIF_OFF = 4 * M_WIDTH
AQ_OFF = MIF_OFF + 2 * M_HEADS
AKV_OFF = AQ_OFF + A_WIDTH
AG_OFF = AKV_OFF + 6 * A_GROUPS * A_HEAD_DIM
IN_COLS = AG_OFF + 3 * A_HEADS
D_MIX = M_WIDTH + A_WIDTH

kernel_name = 'hybrid_mlstm_nsa_peer'


def rmsnorm(a, g):
    a32 = a.astype(jnp.float32)
    r = a32 * lax.rsqrt(jnp.mean(a32 * a32, axis=-1, keepdims=True) + EPS)
    return (r * g.astype(jnp.float32)).astype(a.dtype)


def masked_softmax(s, mask):
    s = jnp.where(mask, s, -jnp.inf)
    m = jnp.max(s, axis=-1, keepdims=True)
    m = jnp.where(jnp.isfinite(m), m, 0.0)
    e = jnp.exp(s - m)
    return e / jnp.maximum(jnp.sum(e, axis=-1, keepdims=True), 1e-30)


def causal_conv(a, w):
    return lax.conv_general_dilated(a, w[:, None, :], window_strides=(1,), padding=[(CONV_W - 1, 0)],
                                    dimension_numbers=('NWC', 'WIO', 'NWC'), feature_group_count=a.shape[-1])


def mlstm_chunkwise(q, k, v, ig, lf):
    b_, s_, h_, d_ = q.shape
    nc = s_ // M_CHUNK

    def chunks(a):
        a = a.reshape((b_, nc, M_CHUNK, h_) + a.shape[3:])
        return jnp.moveaxis(jnp.moveaxis(a, 1, 0), 2, 3)

    causal = jnp.tril(jnp.ones((M_CHUNK, M_CHUNK), dtype=bool))

    def step(carry, inp):
        c_st, n_st, m_st = carry
        qc, kc, vc, ic, fc = inp
        b = jnp.cumsum(fc, axis=-1)
        dmat = jnp.where(causal, b[..., :, None] - b[..., None, :] + ic[..., None, :], -jnp.inf)
        m_inter = b + m_st[..., None]
        m_t = jnp.maximum(m_inter, jnp.max(dmat, axis=-1))
        w = jnp.exp(dmat - m_t[..., None]) * jnp.einsum('bhtd,bhsd->bhts', qc, kc)
        a_inter = jnp.exp(m_inter - m_t)
        num = jnp.einsum('bhts,bhsd->bhtd', w, vc) + a_inter[..., None] * jnp.einsum('bhtd,bhde->bhte', qc, c_st)
        den = jnp.sum(w, axis=-1) + a_inter * jnp.einsum('bhtd,bhd->bht', qc, n_st)
        h = num / jnp.maximum(jnp.abs(den), jnp.exp(-m_t))[..., None]
        b_last = b[..., -1]
        a_s = b_last[..., None] - b + ic
        m_new = jnp.maximum(b_last + m_st, jnp.max(a_s, axis=-1))
        w_s = jnp.exp(a_s - m_new[..., None])
        decay = jnp.exp(b_last + m_st - m_new)
        c_new = decay[..., None, None] * c_st + jnp.einsum('bhs,bhsd,bhse->bhde', w_s, kc, vc)
        n_new = decay[..., None] * n_st + jnp.einsum('bhs,bhsd->bhd', w_s, kc)
        return (c_new, n_new, m_new), h

    init = (jnp.zeros((b_, h_, d_, d_), jnp.float32), jnp.zeros((b_, h_, d_), jnp.float32),
            jnp.zeros((b_, h_), jnp.float32))
    _, hs = lax.scan(step, init, (chunks(q), chunks(k), chunks(v), chunks(ig), chunks(lf)))
    return jnp.moveaxis(hs, 0, 1).transpose(0, 1, 3, 2, 4).reshape(b_, s_, h_, d_)


def nsa(qg, k_c, v_c, k_sel, v_sel, k_win, v_win, gates, pos_k, pos_v, k_w1, k_w2, v_w1, v_w2, kcmp_g):
    f32 = jnp.float32
    b_, s_ = qg.shape[:2]
    n_cmp = (s_ - CMP_BLOCK) // CMP_STRIDE + 1
    n_blk = s_ // SEL_BLOCK
    n_sel = min(SEL_TOP, n_blk)
    scale = A_HEAD_DIM ** -0.5
    cmp_idx = np.arange(n_cmp)[:, None] * CMP_STRIDE + np.arange(CMP_BLOCK)[None, :]
    cmp_end = jnp.asarray(cmp_idx[:, -1], f32)
    starts = np.arange(n_cmp)[:, None] * CMP_STRIDE
    blk_starts = np.arange(n_blk)[None, :] * SEL_BLOCK
    overlap = jnp.asarray((starts < blk_starts + SEL_BLOCK) & (starts + CMP_BLOCK > blk_starts), f32)
    slopes = jnp.asarray(2.0 ** (-8.0 * (np.arange(A_HEADS) + 1) / A_HEADS), f32).reshape(A_GROUPS, A_HPG)
    sl = slopes[None, :, :, None, None]

    def compress(a, pos, w1, w2):
        blk = a[:, cmp_idx] + pos[:, None, :]
        blk = jnp.swapaxes(blk, 2, 3).reshape(b_, n_cmp, A_GROUPS, CMP_BLOCK * A_HEAD_DIM)
        return jax.nn.gelu(blk @ w1) @ w2

    k_cmp = rmsnorm(compress(k_c, pos_k, k_w1, k_w2), kcmp_g)
    v_cmp = compress(v_c, pos_v, v_w1, v_w2)
    ks_blocks = k_sel.reshape(b_, n_blk, SEL_BLOCK, A_GROUPS, A_HEAD_DIM).transpose(0, 3, 1, 2, 4)
    vs_blocks = v_sel.reshape(b_, n_blk, SEL_BLOCK, A_GROUPS, A_HEAD_DIM).transpose(0, 3, 1, 2, 4)
    pad = ((0, 0), (WINDOW, 0), (0, 0), (0, 0))
    kw_pad = jnp.pad(k_win, pad)
    vw_pad = jnp.pad(v_win, pad)
    gather = jax.vmap(jax.vmap(lambda tab, ix: tab[ix]))
    blk_ids = jnp.arange(n_blk)

    def block_fn(qb):
        q0 = qb * Q_BLOCK
        t = q0 + jnp.arange(Q_BLOCK)
        qq = lax.dynamic_slice_in_dim(qg, q0, Q_BLOCK, axis=1)
        gt = lax.dynamic_slice_in_dim(gates, q0, Q_BLOCK, axis=1)
        dist = t.astype(f32)[:, None] - cmp_end[None, :]
        s = jnp.einsum('btghd,bigd->bghti', qq, k_cmp, preferred_element_type=f32) * scale - sl * dist
        p_cmp = masked_softmax(s, dist >= 0)
        o_cmp = jnp.einsum('bghti,bigd->btghd', p_cmp, v_cmp)
        imp = jnp.einsum('bghti,ij->bgtj', p_cmp, overlap)
        cur = (t // SEL_BLOCK)[:, None]
        forced = (blk_ids[None, :] == 0) | (blk_ids[None, :] == cur) | (blk_ids[None, :] == cur - 1)
        imp = jnp.where(forced, FORCE_SCORE, jnp.where(blk_ids[None, :] <= cur, imp, -FORCE_SCORE))
        _, sel = lax.top_k(imp, n_sel)
        k_g = gather(ks_blocks, sel).reshape(b_, A_GROUPS, Q_BLOCK, n_sel * SEL_BLOCK, A_HEAD_DIM)
        v_g = gather(vs_blocks, sel).reshape(b_, A_GROUPS, Q_BLOCK, n_sel * SEL_BLOCK, A_HEAD_DIM)
        pos = (sel[..., None] * SEL_BLOCK + jnp.arange(SEL_BLOCK)).reshape(b_, A_GROUPS, Q_BLOCK, n_sel * SEL_BLOCK)
        dist = (t[None, None, :, None] - pos)[:, :, None]
        s = jnp.einsum('btghd,bgtjd->bghtj', qq, k_g, preferred_element_type=f32) * scale - sl * dist.astype(f32)
        p = masked_softmax(s, dist >= 0)
        o_sel = jnp.einsum('bghtj,bgtjd->btghd', p, v_g)
        kw = lax.dynamic_slice_in_dim(kw_pad, q0, Q_BLOCK + WINDOW, axis=1)
        vw = lax.dynamic_slice_in_dim(vw_pad, q0, Q_BLOCK + WINDOW, axis=1)
        kpos = q0 - WINDOW + jnp.arange(Q_BLOCK + WINDOW)
        dist = t[:, None] - kpos[None, :]
        mask = (kpos[None, :] >= 0) & (dist >= 0) & (dist < WINDOW)
        s = jnp.einsum('btghd,bsgd->bghts', qq, kw, preferred_element_type=f32) * scale - sl * dist.astype(f32)
        p = masked_softmax(s, mask)
        o_win = jnp.einsum('bghts,bsgd->btghd', p, vw)
        return gt[..., 0:1] * o_cmp + gt[..., 1:2] * o_sel + gt[..., 2:3] * o_win

    outs = lax.map(block_fn, jnp.arange(s_ // Q_BLOCK))
    return jnp.moveaxis(outs, 0, 1).reshape(b_, s_, A_WIDTH)


def peer(h, wq, subkeys, u_tab, v_tab):
    b_, s_, d_ = h.shape
    n_tok = b_ * s_
    xt = h.reshape(n_tok, d_)
    q = (xt @ wq).reshape(n_tok, P_HEADS, 2, P_HALF)
    sc = jnp.einsum('nhpd,pkd->nhpk', q, subkeys, preferred_element_type=jnp.float32)
    s1, i1 = lax.top_k(sc[:, :, 0], P_TOPK)
    s2, i2 = lax.top_k(sc[:, :, 1], P_TOPK)
    cand = (s1[..., :, None] + s2[..., None, :]).reshape(n_tok, P_HEADS, P_TOPK * P_TOPK)
    cidx = (i1[..., :, None] * P_NKEYS + i2[..., None, :]).reshape(n_tok, P_HEADS, P_TOPK * P_TOPK)
    top, sel = lax.top_k(cand, P_TOPK)
    eidx = jnp.take_along_axis(cidx, sel, axis=-1)
    gate = jax.nn.softmax(top, axis=-1)
    nb = n_tok // P_TOK_BLOCK

    def blk(args):
        xb, eb, gb = args
        a = jnp.einsum('td,thkd->thk', xb, u_tab[eb], preferred_element_type=jnp.float32)
        w = gb * jax.nn.gelu(a)
        return jnp.einsum('thk,thkd->td', w, v_tab[eb])

    y = lax.map(blk, (xt.reshape(nb, P_TOK_BLOCK, d_), eidx.reshape(nb, P_TOK_BLOCK, P_HEADS, P_TOPK),
                      gate.reshape(nb, P_TOK_BLOCK, P_HEADS, P_TOPK)))
    return y.reshape(b_, s_, d_)


def setup_inputs(seed: int = 0) -> dict:
    key = jax.random.key(seed)
    ks = jax.random.split(key, 24)
    L = DEPTH
    f32 = jnp.float32

    def nrm(k, shape, scale):
        return jax.random.normal(k, shape, f32) * scale

    return {
        'x': nrm(ks[0], (BATCH, SEQ, D_MODEL), 1.0),
        'c': nrm(ks[1], (BATCH, D_MODEL), 1.0),
        'w_mod': nrm(ks[2], (L, D_MODEL, 6 * D_MODEL), 0.5 * D_MODEL ** -0.5),
        'b_mod': nrm(ks[3], (L, 6 * D_MODEL), 0.02),
        'norm1_g': 1.0 + nrm(ks[4], (L, D_MODEL), 0.05),
        'norm2_g': 1.0 + nrm(ks[5], (L, D_MODEL), 0.05),
        'w_in': nrm(ks[6], (L, D_MODEL, IN_COLS), D_MODEL ** -0.5),
        'conv_qk': nrm(ks[7], (L, CONV_W, 2 * M_WIDTH), CONV_W ** -0.5),
        'b_igate': nrm(ks[8], (L, M_HEADS), 0.1),
        'b_fgate': jnp.linspace(3.0, 6.0, M_HEADS, dtype=f32)[None, :] + nrm(ks[9], (L, M_HEADS), 0.1),
        'mlstm_norm_g': 1.0 + nrm(ks[10], (L, M_WIDTH), 0.05),
        'qn_g': 1.0 + nrm(ks[11], (L, A_HEAD_DIM), 0.05),
        'kn_g': 1.0 + nrm(ks[12], (L, 3, A_HEAD_DIM), 0.05),
        'cmp_pos_k': nrm(ks[13], (L, CMP_BLOCK, A_HEAD_DIM), 0.2),
        'cmp_pos_v': nrm(ks[14], (L, CMP_BLOCK, A_HEAD_DIM), 0.2),
        'cmp_k_w1': nrm(ks[15], (L, CMP_BLOCK * A_HEAD_DIM, CMP_HIDDEN), (CMP_BLOCK * A_HEAD_DIM) ** -0.5),
        'cmp_k_w2': nrm(ks[16], (L, CMP_HIDDEN, A_HEAD_DIM), CMP_HIDDEN ** -0.5),
        'cmp_v_w1': nrm(ks[17], (L, CMP_BLOCK * A_HEAD_DIM, CMP_HIDDEN), (CMP_BLOCK * A_HEAD_DIM) ** -0.5),
        'cmp_v_w2': nrm(ks[18], (L, CMP_HIDDEN, A_HEAD_DIM), CMP_HIDDEN ** -0.5),
        'w_out': nrm(ks[19], (L, D_MIX, D_MODEL), D_MIX ** -0.5),
        'peer_wq': nrm(ks[20], (L, D_MODEL, P_HEADS * P_QDIM), D_MODEL ** -0.5),
        'peer_subkeys': nrm(ks[21], (L, 2, P_NKEYS, P_HALF), P_HALF ** -0.5),
        'peer_u': nrm(ks[22], (L, P_EXPERTS, D_MODEL), D_MODEL ** -0.5),
        'peer_v': nrm(ks[23], (L, P_EXPERTS, D_MODEL), 1.0),
    }


def reference(x, c, w_mod, b_mod, norm1_g, norm2_g, w_in, conv_qk, b_igate, b_fgate, mlstm_norm_g,
              qn_g, kn_g, cmp_pos_k, cmp_pos_v, cmp_k_w1, cmp_k_w2, cmp_v_w1, cmp_v_w2, w_out,
              peer_wq, peer_subkeys, peer_u, peer_v):
    f32 = jnp.float32
    b_, s_, _ = x.shape
    for l in range(DEPTH):
        mod = (jax.nn.silu(c) @ w_mod[l] + b_mod[l]).reshape(b_, 6, 1, D_MODEL)
        shift1, scale1, gate1, shift2, scale2, gate2 = (mod[:, i] for i in range(6))
        h = rmsnorm(x, norm1_g[l]) * (1.0 + scale1) + shift1
        z = h @ w_in[l]
        qk = jax.nn.silu(causal_conv(z[..., :MV_OFF], conv_qk[l]))
        mshape = (b_, s_, M_HEADS, M_HEAD_DIM)
        mq = qk[..., :M_WIDTH].reshape(mshape).astype(f32)
        mk = qk[..., M_WIDTH:].reshape(mshape).astype(f32) * (M_HEAD_DIM ** -0.5)
        mv = z[..., MV_OFF:MO_OFF].reshape(mshape).astype(f32)
        mo = jax.nn.sigmoid(z[..., MO_OFF:MIF_OFF].astype(f32)).reshape(mshape)
        ig = (z[..., MIF_OFF:MIF_OFF + M_HEADS] + b_igate[l]).astype(f32)
        lf = jax.nn.log_sigmoid((z[..., MIF_OFF + M_HEADS:AQ_OFF] + b_fgate[l]).astype(f32))
        hm = mlstm_chunkwise(mq, mk, mv, ig, lf)
        hm = (rmsnorm(hm, mlstm_norm_g[l].reshape(M_HEADS, M_HEAD_DIM)) * mo).reshape(b_, s_, M_WIDTH)
        aq = rmsnorm(z[..., AQ_OFF:AKV_OFF].reshape(b_, s_, A_HEADS, A_HEAD_DIM), qn_g[l])
        aq = aq.reshape(b_, s_, A_GROUPS, A_HPG, A_HEAD_DIM)
        kv = z[..., AKV_OFF:AG_OFF].reshape(b_, s_, 6, A_GROUPS, A_HEAD_DIM)
        gates = jax.nn.sigmoid(z[..., AG_OFF:IN_COLS].astype(f32)).reshape(b_, s_, A_GROUPS, A_HPG, 3)
        ha = nsa(aq, kv[:, :, 0], kv[:, :, 1], rmsnorm(kv[:, :, 2], kn_g[l, 1]), kv[:, :, 3],
                 rmsnorm(kv[:, :, 4], kn_g[l, 2]), kv[:, :, 5], gates, cmp_pos_k[l], cmp_pos_v[l],
                 cmp_k_w1[l], cmp_k_w2[l], cmp_v_w1[l], cmp_v_w2[l], kn_g[l, 0])
        y = jnp.concatenate([hm, ha], axis=-1).astype(x.dtype) @ w_out[l]
        x = x + gate1 * y
        h2 = rmsnorm(x, norm2_g[l]) * (1.0 + scale2) + shift2
        x = x + gate2 * peer(h2, peer_wq[l], peer_subkeys[l], peer_u[l], peer_v[l]).astype(x.dtype)
    return x
```

```python
import functools

import numpy as np
import jax
import jax.numpy as jnp
from jax import lax
from jax.experimental import pallas as pl
from jax.experimental.pallas import tpu as pltpu

F32 = jnp.float32
BF16 = jnp.bfloat16
HIGHEST = lax.Precision.HIGHEST

EPS = 1e-6
D_MODEL = 2048
M_HEADS = 4
M_HEAD_DIM = 256
M_WIDTH = M_HEADS * M_HEAD_DIM
M_CHUNK = 256
CONV_W = 4
A_GROUPS = 4
A_HPG = 4
A_HEADS = A_GROUPS * A_HPG
A_HEAD_DIM = 64
A_WIDTH = A_HEADS * A_HEAD_DIM
CMP_BLOCK = 32
CMP_STRIDE = 16
CMP_HIDDEN = 256
SEL_BLOCK = 64
SEL_TOP = 16
WINDOW = 512
Q_BLOCK = 128
FORCE_SCORE = 1e4
P_HEADS = 8
P_NKEYS = 128
P_TOPK = 16
P_QDIM = 256
P_HALF = 128
MV_OFF = 2 * M_WIDTH
MO_OFF = 3 * M_WIDTH
MIF_OFF = 4 * M_WIDTH
AQ_OFF = MIF_OFF + 2 * M_HEADS
AKV_OFF = AQ_OFF + A_WIDTH
AG_OFF = AKV_OFF + 6 * A_GROUPS * A_HEAD_DIM
IN_COLS = AG_OFF + 3 * A_HEADS
Z_M = 0
Z_AQ = 4096
Z_KV = 5120
Z_IF = 6656
Z_AG = 6784
Z_COLS = 6912
LANES = 128
NEG = -1e30
VMEM_LIMIT = 56 * 1024 * 1024


def _cparams(*sem):
    return pltpu.CompilerParams(dimension_semantics=sem, vmem_limit_bytes=VMEM_LIMIT)


def _gelu_tanh(x):
    return 0.5 * x * (1.0 + jnp.tanh(np.sqrt(2.0 / np.pi).astype(np.float32) * (x + 0.044715 * (x * x * x))))


def _mod_kernel(c_ref, w_ref, b_ref, o_ref):
    c = c_ref[...]
    s = c * jax.nn.sigmoid(c)
    o_ref[...] = jnp.dot(s, w_ref[...], preferred_element_type=F32, precision=HIGHEST) + b_ref[...]


def _modulation(c, w_mod, b_mod):
    b_, d = c.shape
    n = w_mod.shape[1]
    tn = 1024
    cp = jnp.pad(c, ((0, 8 - b_), (0, 0)))
    out = pl.pallas_call(
        _mod_kernel,
        grid=(n // tn,),
        in_specs=[pl.BlockSpec((8, d), lambda j: (0, 0)),
                  pl.BlockSpec((d, tn), lambda j: (0, j)),
                  pl.BlockSpec((1, tn), lambda j: (0, j))],
        out_specs=pl.BlockSpec((8, tn), lambda j: (0, j)),
        out_shape=jax.ShapeDtypeStruct((8, n), F32),
        compiler_params=_cparams("parallel"),
        name="modulation",
    )(cp, w_mod, b_mod.reshape(1, n))
    return out[:b_]


def _inproj_kernel(x_ref, g_ref, sc_ref, sh_ref, w_ref, o_ref, h_scr):
    @pl.when(pl.program_id(1) == 0)
    def _():
        x = x_ref[...]
        r = x * lax.rsqrt(jnp.mean(x * x, axis=-1, keepdims=True) + EPS)
        h = (r * g_ref[...]) * (1.0 + sc_ref[0]) + sh_ref[0]
        h_scr[...] = h.astype(BF16)

    o_ref[...] = jnp.dot(h_scr[...], w_ref[...], preferred_element_type=F32)


def _in_projection(x2, g1, scale1, shift1, w_pad, seq):
    n, d = x2.shape
    zc = w_pad.shape[1]
    tm = min(512, seq)
    tn = 768
    return pl.pallas_call(
        _inproj_kernel,
        grid=(n // tm, zc // tn),
        in_specs=[pl.BlockSpec((tm, d), lambda i, j: (i, 0)),
                  pl.BlockSpec((1, d), lambda i, j: (0, 0)),
                  pl.BlockSpec((1, 1, d), lambda i, j: ((i * tm) // seq, 0, 0)),
                  pl.BlockSpec((1, 1, d), lambda i, j: ((i * tm) // seq, 0, 0)),
                  pl.BlockSpec((d, tn), lambda i, j: (0, j))],
        out_specs=pl.BlockSpec((tm, tn), lambda i, j: (i, j)),
        out_shape=jax.ShapeDtypeStruct((n, zc), F32),
        scratch_shapes=[pltpu.VMEM((tm, d), BF16)],
        compiler_params=_cparams("parallel", "arbitrary"),
        name="in_projection",
    )(x2, g1.reshape(1, d), scale1, shift1, w_pad)


def _pad_in_weights(w_in):
    d = w_in.shape[0]
    z = lambda k: jnp.zeros((d, k), w_in.dtype)
    w = jnp.concatenate([
        w_in[:, :MIF_OFF],
        w_in[:, AQ_OFF:AKV_OFF],
        w_in[:, AKV_OFF:AG_OFF],
        w_in[:, MIF_OFF:AQ_OFF], z(LANES - 2 * M_HEADS),
        w_in[:, AG_OFF:IN_COLS], z(LANES - 3 * A_HEADS),
    ], axis=1)
    return w.astype(BF16)


def _mlstm_kernel(q_ref, k_ref, v_ref, o_ref, if_ref, cq_ref, ck_ref, bias_ref, g_ref, out_ref,
                  c_scr, n_scr, m_scr, qt_scr, kt_scr):
    L, d = q_ref.shape[1], q_ref.shape[2]
    h = pl.program_id(1)

    @pl.when(pl.program_id(2) == 0)
    def _():
        c_scr[...] = jnp.zeros_like(c_scr)
        n_scr[...] = jnp.zeros_like(n_scr)
        m_scr[...] = jnp.zeros_like(m_scr)
        qt_scr[...] = jnp.zeros_like(qt_scr)
        kt_scr[...] = jnp.zeros_like(kt_scr)

    def conv_silu(raw, tail_scr, w_ref):
        ext = jnp.concatenate([tail_scr[...], raw], axis=0)
        acc = jnp.zeros((L, d), F32)
        for j in range(CONV_W):
            off = 8 - (CONV_W - 1) + j
            acc = acc + w_ref[j:j + 1, :] * ext[off:off + L, :]
        tail_scr[...] = raw[L - 8:, :]
        return acc * jax.nn.sigmoid(acc)

    qc = conv_silu(q_ref[0], qt_scr, cq_ref)
    kc = conv_silu(k_ref[0], kt_scr, ck_ref) * (d ** -0.5)
    vc = v_ref[0]

    gt = if_ref[0] + bias_ref[...]
    lane = lax.broadcasted_iota(jnp.int32, (L, LANES), 1)
    logsig = jnp.minimum(gt, 0.0) - jnp.log(1.0 + jnp.exp(-jnp.abs(gt)))
    gt = jnp.where(lane < M_HEADS, gt, logsig)
    rows = lax.broadcasted_iota(jnp.int32, (L, L), 0)
    cols = lax.broadcasted_iota(jnp.int32, (L, L), 1)
    causal = rows >= cols
    tril = jnp.where(causal, 1.0, 0.0).astype(F32)
    csum = jnp.dot(tril, gt, preferred_element_type=F32, precision=HIGHEST)
    gt_t = gt.T
    csum_t = csum.T
    sub = lax.broadcasted_iota(jnp.int32, (LANES, L), 0)
    ig_col = jnp.sum(jnp.where(lane == h, gt, 0.0), axis=1, keepdims=True)
    b_col = jnp.sum(jnp.where(lane == h + M_HEADS, csum, 0.0), axis=1, keepdims=True)
    ig_row = jnp.sum(jnp.where(sub == h, gt_t, 0.0), axis=0, keepdims=True)
    b_row = jnp.sum(jnp.where(sub == h + M_HEADS, csum_t, 0.0), axis=0, keepdims=True)

    m_st = m_scr[0:1, 0:1]
    dmat = jnp.where(causal, b_col - b_row + ig_row, NEG)
    m_inter = b_col + m_st
    m_t = jnp.maximum(m_inter, jnp.max(dmat, axis=1, keepdims=True))
    qb = qc.astype(BF16)
    kb = kc.astype(BF16)
    vb = vc.astype(BF16)
    qk = lax.dot_general(qb, kb, (((1,), (1,)), ((), ())), preferred_element_type=F32)
    w = jnp.exp(dmat - m_t) * qk
    a_inter = jnp.exp(m_inter - m_t)
    c_st = c_scr[...]
    n_st = n_scr[...]
    num = (jnp.dot(w.astype(BF16), vb, preferred_element_type=F32)
           + a_inter * jnp.dot(qb, c_st.astype(BF16), preferred_element_type=F32))
    den = jnp.sum(w, axis=1, keepdims=True) + a_inter * jnp.sum(qc * n_st, axis=1, keepdims=True)
    hout = num / jnp.maximum(jnp.abs(den), jnp.exp(-m_t))

    b_last = b_col[L - 1:L, :]
    a_s = b_last - b_col + ig_col
    m_new = jnp.maximum(b_last + m_st, jnp.max(a_s, axis=0, keepdims=True))
    w_s = jnp.exp(a_s - m_new)
    decay = jnp.exp(b_last + m_st - m_new)
    kw = kc * w_s
    c_scr[...] = decay * c_st + lax.dot_general(kw.astype(BF16), vb, (((0,), (0,)), ((), ())),
                                                preferred_element_type=F32)
    n_scr[...] = decay * n_st + jnp.sum(kw, axis=0, keepdims=True)
    m_scr[...] = jnp.broadcast_to(m_new, m_scr.shape)

    r = hout * lax.rsqrt(jnp.mean(hout * hout, axis=-1, keepdims=True) + EPS)
    out_ref[0] = (r * g_ref[...]) * jax.nn.sigmoid(o_ref[0])


def _mlstm(z3, conv_qk, gate_bias, norm_g):
    b_, s_, _ = z3.shape
    L = min(M_CHUNK, s_)
    d = M_HEAD_DIM
    H = M_HEADS
    blk = lambda off: pl.BlockSpec((1, L, d), lambda b, h, c: (b, c, off + h))
    return pl.pallas_call(
        _mlstm_kernel,
        grid=(b_, H, s_ // L),
        in_specs=[blk(0), blk(H), blk(2 * H), blk(3 * H),
                  pl.BlockSpec((1, L, LANES), lambda b, h, c: (b, c, Z_IF // LANES)),
                  pl.BlockSpec((CONV_W, d), lambda b, h, c: (0, h)),
                  pl.BlockSpec((CONV_W, d), lambda b, h, c: (0, H + h)),
                  pl.BlockSpec((1, LANES), lambda b, h, c: (0, 0)),
                  pl.BlockSpec((1, d), lambda b, h, c: (0, h))],
        out_specs=pl.BlockSpec((1, L, d), lambda b, h, c: (b, c, h)),
        out_shape=jax.ShapeDtypeStruct((b_, s_, M_WIDTH), F32),
        scratch_shapes=[pltpu.VMEM((d, d), F32), pltpu.VMEM((1, d), F32), pltpu.VMEM((8, LANES), F32),
                        pltpu.VMEM((8, d), F32), pltpu.VMEM((8, d), F32)],
        compiler_params=_cparams("parallel", "parallel", "arbitrary"),
        name="mlstm",
    )(z3, z3, z3, z3, z3, conv_qk, conv_qk, gate_bias, norm_g.reshape(1, M_WIDTH))


def _nsa_prep_kernel(aq_ref, kc_ref, vc_ref, ks_ref, vs_ref, kw_ref, vw_ref, qg_ref, kg_ref,
                     qn_o, kc_o, vc_o, ks_o, vs_o, kw_o, vw_o):
    hd = A_HEAD_DIM

    def norm(xh, g):
        return (xh * lax.rsqrt(jnp.mean(xh * xh, axis=-1, keepdims=True) + EPS)) * g

    aq = aq_ref[0]
    qg = qg_ref[...]
    for h in range(A_HEADS):
        xh = aq[:, h * hd:(h + 1) * hd]
        qn_o[0, h // A_HPG, h % A_HPG] = (norm(xh, qg) * (hd ** -0.5)).astype(BF16)
    kc, vc, ks, vs, kw, vw = kc_ref[0], vc_ref[0], ks_ref[0], vs_ref[0], kw_ref[0], vw_ref[0]
    for g in range(A_GROUPS):
        sl = slice(g * hd, (g + 1) * hd)
        kc_o[0, g] = kc[:, sl]
        vc_o[0, g] = vc[:, sl]
        ks_o[0, g] = norm(ks[:, sl], kg_ref[1:2, :]).astype(BF16)
        vs_o[0, g] = vs[:, sl].astype(BF16)
        kw_o[0, g] = norm(kw[:, sl], kg_ref[2:3, :]).astype(BF16)
        vw_o[0, g] = vw[:, sl].astype(BF16)


def _nsa_prep(z3, qn_g, kn_g):
    b_, s_, _ = z3.shape
    ts = min(512, s_)
    G, hd = A_GROUPS, A_HEAD_DIM
    kvw = G * hd
    kv_in = lambda i: pl.BlockSpec((1, ts, kvw), lambda b, t: (b, t, Z_KV // kvw + i))
    kv_out = pl.BlockSpec((1, G, ts, hd), lambda b, t: (b, 0, t, 0))
    kv_shape = lambda dt: jax.ShapeDtypeStruct((b_, G, s_, hd), dt)
    return pl.pallas_call(
        _nsa_prep_kernel,
        grid=(b_, s_ // ts),
        in_specs=[pl.BlockSpec((1, ts, A_WIDTH), lambda b, t: (b, t, Z_AQ // A_WIDTH))]
                 + [kv_in(i) for i in range(6)]
                 + [pl.BlockSpec((1, hd), lambda b, t: (0, 0)), pl.BlockSpec((3, hd), lambda b, t: (0, 0))],
        out_specs=[pl.BlockSpec((1, G, A_HPG, ts, hd), lambda b, t: (b, 0, 0, t, 0))] + [kv_out] * 6,
        out_shape=[jax.ShapeDtypeStruct((b_, G, A_HPG, s_, hd), BF16),
                   kv_shape(F32), kv_shape(F32), kv_shape(BF16), kv_shape(BF16), kv_shape(BF16), kv_shape(BF16)],
        compiler_params=_cparams("parallel", "parallel"),
        name="nsa_prep",
    )(z3, z3, z3, z3, z3, z3, z3, qn_g.reshape(1, hd), kn_g)


def _compress_kernel(r_ref, pos_ref, w1a_ref, w1b_ref, w2_ref, g_ref, o_ref, *, do_norm):
    r = r_ref[0, 0]
    nr = r.shape[0]
    u = jnp.dot((r + pos_ref[0:1, :]).astype(BF16), w1a_ref[...], preferred_element_type=F32)
    v = jnp.dot((r + pos_ref[1:2, :]).astype(BF16), w1b_ref[...], preferred_element_type=F32)
    pre = u + pltpu.roll(v, nr - 1, 0)
    out = jnp.dot(_gelu_tanh(pre).astype(BF16), w2_ref[...], preferred_element_type=F32)
    if do_norm:
        out = (out * lax.rsqrt(jnp.mean(out * out, axis=-1, keepdims=True) + EPS)) * g_ref[...]
    o_ref[0, 0] = out.astype(BF16)


def _compress(a, pos, w1, w2, g, do_norm):
    b_, G, s_, hd = a.shape
    nr = s_ // CMP_STRIDE
    half = CMP_STRIDE * hd
    r = a.reshape(b_, G, nr, half)
    w1b16 = w1.astype(BF16)
    return pl.pallas_call(
        functools.partial(_compress_kernel, do_norm=do_norm),
        grid=(b_, G),
        in_specs=[pl.BlockSpec((1, 1, nr, half), lambda b, g: (b, g, 0, 0)),
                  pl.BlockSpec((2, half), lambda b, g: (0, 0)),
                  pl.BlockSpec((half, CMP_HIDDEN), lambda b, g: (0, 0)),
                  pl.BlockSpec((half, CMP_HIDDEN), lambda b, g: (1, 0)),
                  pl.BlockSpec((CMP_HIDDEN, hd), lambda b, g: (0, 0)),
                  pl.BlockSpec((1, hd), lambda b, g: (0, 0))],
        out_specs=pl.BlockSpec((1, 1, nr, hd), lambda b, g: (b, g, 0, 0)),
        out_shape=jax.ShapeDtypeStruct((b_, G, nr, hd), BF16),
        compiler_params=_cparams("parallel", "parallel"),
        name="nsa_compress_norm" if do_norm else "nsa_compress",
    )(r, pos.reshape(2, half), w1b16, w1b16, w2.astype(BF16), g.reshape(1, hd))


def _nsa_cmp_kernel(q_ref, kc_ref, vc_ref, ov_ref, slope_ref, ocmp_ref, sel_ref, *, n_cmp, n_blk, n_sel):
    T, hd = q_ref.shape[3], q_ref.shape[4]
    R = A_HPG * T
    nc = kc_ref.shape[2]
    q0 = pl.program_id(2) * T
    q = q_ref[0, 0].reshape(R, hd)
    s = lax.dot_general(q, kc_ref[0, 0], (((1,), (1,)), ((), ())), preferred_element_type=F32)
    row = lax.broadcasted_iota(jnp.int32, (R, nc), 0)
    ci = lax.broadcasted_iota(jnp.int32, (R, nc), 1)
    t = q0 + (row & (T - 1))
    disti = t - (ci * CMP_STRIDE + (CMP_BLOCK - 1))
    valid = (disti >= 0) & (ci < n_cmp)
    slope = slope_ref[0][:, 0:1]
    s = jnp.where(valid, s - slope * disti.astype(F32), NEG)
    m = jnp.max(s, axis=1, keepdims=True)
    e = jnp.where(valid, jnp.exp(s - m), 0.0)
    p = e / jnp.maximum(jnp.sum(e, axis=1, keepdims=True), 1e-30)
    oc = jnp.dot(p.astype(BF16), vc_ref[0, 0], preferred_element_type=F32)
    for h in range(A_HPG):
        ocmp_ref[0, :, h * hd:(h + 1) * hd] = oc[h * T:(h + 1) * T, :]
    ps = p[0:T] + p[T:2 * T] + p[2 * T:3 * T] + p[3 * T:4 * T]
    hi = ps.astype(BF16)
    lo = (ps - hi.astype(F32)).astype(BF16)
    ov = ov_ref[...]
    imp = jnp.dot(hi, ov, preferred_element_type=F32) + jnp.dot(lo, ov, preferred_element_type=F32)
    blk = lax.broadcasted_iota(jnp.int32, (T, LANES), 1)
    cur = (q0 + lax.broadcasted_iota(jnp.int32, (T, LANES), 0)) // SEL_BLOCK
    forced = (blk == 0) | (blk == cur) | (blk == cur - 1)
    imp = jnp.where(forced, FORCE_SCORE, jnp.where(blk <= cur, imp, -FORCE_SCORE))
    imp = jnp.where(blk < n_blk, imp, -3e38)
    sel = jnp.zeros((T, LANES), F32)
    for _ in range(n_sel):
        mx = jnp.max(imp, axis=1, keepdims=True)
        idx = jnp.min(jnp.where(imp == mx, blk, LANES), axis=1, keepdims=True)
        pick = blk == idx
        sel = jnp.where(pick, 1.0, sel)
        imp = jnp.where(pick, -3.4e38, imp)
    sel_ref[0, 0] = sel.astype(BF16)


def _slope_table():
    slopes = (2.0 ** (-8.0 * (np.arange(A_HEADS) + 1) / A_HEADS)).astype(np.float32).reshape(A_GROUPS, A_HPG)
    tab = np.repeat(slopes, Q_BLOCK, axis=1)
    return jnp.asarray(np.broadcast_to(tab[:, :, None], (A_GROUPS, A_HPG * Q_BLOCK, LANES)).copy())


def _overlap_table(s_, nc):
    n_cmp = (s_ - CMP_BLOCK) // CMP_STRIDE + 1
    n_blk = s_ // SEL_BLOCK
    starts = np.arange(nc)[:, None] * CMP_STRIDE
    blk_starts = np.arange(LANES)[None, :] * SEL_BLOCK
    ov = (starts < blk_starts + SEL_BLOCK) & (starts + CMP_BLOCK > blk_starts)
    ov &= (np.arange(nc)[:, None] < n_cmp) & (np.arange(LANES)[None, :] < n_blk)
    return jnp.asarray(ov.astype(np.float32)).astype(BF16)


def _nsa_cmp(qn, k_cmp, v_cmp):
    b_, G, hpg, s_, hd = qn.shape
    T = Q_BLOCK
    nc = k_cmp.shape[2]
    n_cmp = (s_ - CMP_BLOCK) // CMP_STRIDE + 1
    n_blk = s_ // SEL_BLOCK
    kern = functools.partial(_nsa_cmp_kernel, n_cmp=n_cmp, n_blk=n_blk, n_sel=min(SEL_TOP, n_blk))
    return pl.pallas_call(
        kern,
        grid=(b_, G, s_ // T),
        in_specs=[pl.BlockSpec((1, 1, hpg, T, hd), lambda b, g, i: (b, g, 0, i, 0)),
                  pl.BlockSpec((1, 1, nc, hd), lambda b, g, i: (b, g, 0, 0)),
                  pl.BlockSpec((1, 1, nc, hd), lambda b, g, i: (b, g, 0, 0)),
                  pl.BlockSpec((nc, LANES), lambda b, g, i: (0, 0)),
                  pl.BlockSpec((1, hpg * T, LANES), lambda b, g, i: (g, 0, 0))],
        out_specs=[pl.BlockSpec((1, T, hpg * hd), lambda b, g, i: (b, i, g)),
                   pl.BlockSpec((1, 1, T, LANES), lambda b, g, i: (b, g, i, 0))],
        out_shape=[jax.ShapeDtypeStruct((b_, s_, A_WIDTH), F32),
                   jax.ShapeDtypeStruct((b_, G, s_, LANES), BF16)],
        compiler_params=_cparams("parallel", "parallel", "parallel"),
        name="nsa_cmp_topk",
    )(qn, k_cmp, v_cmp, _overlap_table(s_, nc), _slope_table())


def _nsa_attn_kernel(q_ref, kw_ref, vw_ref, ks_ref, vs_ref, sel_ref, slope_ref, owin_ref, osel_ref,
                     m_scr, l_scr, acc_scr, *, wk, tk):
    T, hd = q_ref.shape[3], q_ref.shape[4]
    R = A_HPG * T
    qb = pl.program_id(2)
    q0 = qb * T
    q = q_ref[0, 0].reshape(R, hd)
    slope = slope_ref[0][:, 0:1]
    nt = (((1,), (1,)), ((), ()))

    start = pl.multiple_of(jnp.maximum(q0 + T - wk, 0), T)
    kw = kw_ref[0, 0, pl.ds(start, wk), :]
    vw = vw_ref[0, 0, pl.ds(start, wk), :]
    s = lax.dot_general(q, kw, nt, preferred_element_type=F32)
    t = q0 + (lax.broadcasted_iota(jnp.int32, (R, wk), 0) & (T - 1))
    disti = t - (start + lax.broadcasted_iota(jnp.int32, (R, wk), 1))
    mask = (disti >= 0) & (disti < WINDOW)
    s = jnp.where(mask, s - slope * disti.astype(F32), NEG)
    m = jnp.max(s, axis=1, keepdims=True)
    e = jnp.exp(s - m)
    ow = jnp.dot(e.astype(BF16), vw, preferred_element_type=F32) / jnp.sum(e, axis=1, keepdims=True)
    for h in range(A_HPG):
        owin_ref[0, :, h * hd:(h + 1) * hd] = ow[h * T:(h + 1) * T, :]

    sel = sel_ref[0, 0]
    m_scr[...] = jnp.full_like(m_scr, NEG)
    l_scr[...] = jnp.zeros_like(l_scr)
    acc_scr[...] = jnp.zeros_like(acc_scr)
    bpt = tk // SEL_BLOCK
    lane = lax.broadcasted_iota(jnp.int32, (T, LANES), 1)

    def body(j, carry):
        kv0 = pl.multiple_of(j * tk, tk)
        in_tile = (lane >= j * bpt) & (lane < (j + 1) * bpt)
        has = jnp.max(jnp.where(in_tile, sel.astype(F32), 0.0)) > 0.5

        @pl.when(jnp.logical_or(has, j == 0))
        def _():
            bi = lax.broadcasted_iota(jnp.int32, (LANES, tk), 0)
            cc = lax.broadcasted_iota(jnp.int32, (LANES, tk), 1)
            expand = jnp.where(((kv0 + cc) // SEL_BLOCK) == bi, 1.0, 0.0).astype(BF16)
            mexp = jnp.dot(sel, expand, preferred_element_type=F32)
            mexp = jnp.concatenate([mexp] * A_HPG, axis=0)
            k = ks_ref[0, 0, pl.ds(kv0, tk), :]
            v = vs_ref[0, 0, pl.ds(kv0, tk), :]
            sc = lax.dot_general(q, k, nt, preferred_element_type=F32)
            tt = q0 + (lax.broadcasted_iota(jnp.int32, (R, tk), 0) & (T - 1))
            dd = tt - (kv0 + lax.broadcasted_iota(jnp.int32, (R, tk), 1))
            ok = (mexp > 0.5) & (dd >= 0)
            sc = jnp.where(ok, sc - slope * dd.astype(F32), NEG)
            m_old = m_scr[...]
            m_new = jnp.maximum(m_old, jnp.max(sc, axis=1, keepdims=True))
            alpha = jnp.exp(m_old - m_new)
            p = jnp.exp(sc - m_new)
            l_scr[...] = alpha * l_scr[...] + jnp.sum(p, axis=1, keepdims=True)
            acc_scr[...] = alpha * acc_scr[...] + jnp.dot(p.astype(BF16), v, preferred_element_type=F32)
            m_scr[...] = m_new

        return carry

    lax.fori_loop(0, (q0 + T - 1) // tk + 1, body, 0)
    os_ = acc_scr[...] / l_scr[...]
    for h in range(A_HPG):
        osel_ref[0, :, h * hd:(h + 1) * hd] = os_[h * T:(h + 1) * T, :]


def _nsa_attn(qn, k_win, v_win, k_sel, v_sel, sel):
    b_, G, hpg, s_, hd = qn.shape
    T = Q_BLOCK
    wk = min(WINDOW + T, s_)
    tk = min(512, s_)
    full = pl.BlockSpec((1, 1, s_, hd), lambda b, g, i: (b, g, 0, 0))
    out = pl.BlockSpec((1, T, hpg * hd), lambda b, g, i: (b, i, g))
    return pl.pallas_call(
        functools.partial(_nsa_attn_kernel, wk=wk, tk=tk),
        grid=(b_, G, s_ // T),
        in_specs=[pl.BlockSpec((1, 1, hpg, T, hd), lambda b, g, i: (b, g, 0, i, 0)),
                  full, full, full, full,
                  pl.BlockSpec((1, 1, T, LANES), lambda b, g, i: (b, g, i, 0)),
                  pl.BlockSpec((1, hpg * T, LANES), lambda b, g, i: (g, 0, 0))],
        out_specs=[out, out],
        out_shape=[jax.ShapeDtypeStruct((b_, s_, A_WIDTH), F32)] * 2,
        scratch_shapes=[pltpu.VMEM((hpg * T, 1), F32), pltpu.VMEM((hpg * T, 1), F32),
                        pltpu.VMEM((hpg * T, hd), F32)],
        compiler_params=_cparams("parallel", "parallel", "arbitrary"),
        name="nsa_window_selected",
    )(qn, k_win, v_win, k_sel, v_sel, sel, _slope_table())


def _outproj_kernel(hm_ref, oc_ref, os_ref, ow_ref, ag_ref, ex_ref, x_ref, wo_ref, g1_ref, n2_ref, sc2_ref,
                    sh2_ref, x1_ref, h2_ref):
    sg = jax.nn.sigmoid(ag_ref[...])
    hi = sg.astype(BF16)
    lo = (sg - hi.astype(F32)).astype(BF16)
    ha = jnp.zeros(oc_ref.shape, F32)
    for br, o_ref in enumerate((oc_ref, os_ref, ow_ref)):
        ex = ex_ref[br]
        gexp = jnp.dot(hi, ex, preferred_element_type=F32) + jnp.dot(lo, ex, preferred_element_type=F32)
        ha = ha + gexp * o_ref[...]
    mw = hm_ref.shape[1]
    y = (jnp.dot(hm_ref[...].astype(BF16), wo_ref[0:mw, :], preferred_element_type=F32)
         + jnp.dot(ha.astype(BF16), wo_ref[mw:, :], preferred_element_type=F32))
    x1 = x_ref[...] + g1_ref[0] * y
    x1_ref[...] = x1
    r = x1 * lax.rsqrt(jnp.mean(x1 * x1, axis=-1, keepdims=True) + EPS)
    h2_ref[...] = (r * n2_ref[...]) * (1.0 + sc2_ref[0]) + sh2_ref[0]


def _gate_expand_table():
    ex = np.zeros((3, LANES, A_WIDTH), np.float32)
    for hd in range(A_HEADS):
        for br in range(3):
            ex[br, hd * 3 + br, hd * A_HEAD_DIM:(hd + 1) * A_HEAD_DIM] = 1.0
    return jnp.asarray(ex).astype(BF16)


def _out_projection(hm2, oc2, os2, ow2, z2, x2, w_out, gate1, norm2_g, scale2, shift2, seq):
    n, d = x2.shape
    tm = min(256, seq)
    row = lambda w: pl.BlockSpec((tm, w), lambda i: (i, 0))
    per_b = pl.BlockSpec((1, 1, d), lambda i: ((i * tm) // seq, 0, 0))
    return pl.pallas_call(
        _outproj_kernel,
        grid=(n // tm,),
        in_specs=[row(M_WIDTH), row(A_WIDTH), row(A_WIDTH), row(A_WIDTH),
                  pl.BlockSpec((tm, LANES), lambda i: (i, Z_AG // LANES)),
                  pl.BlockSpec((3, LANES, A_WIDTH), lambda i: (0, 0, 0)),
                  row(d),
                  pl.BlockSpec((M_WIDTH + A_WIDTH, d), lambda i: (0, 0)),
                  per_b,
                  pl.BlockSpec((1, d), lambda i: (0, 0)),
                  per_b, per_b],
        out_specs=[row(d), row(d)],
        out_shape=[jax.ShapeDtypeStruct((n, d), F32)] * 2,
        compiler_params=_cparams("parallel"),
        name="out_projection",
    )(hm2, oc2, os2, ow2, z2, _gate_expand_table(), x2, w_out.astype(BF16), gate1, norm2_g.reshape(1, d),
      scale2, shift2)


def _peer_route_kernel(h2_ref, wqt_ref, sk_ref, eidx_ref, gate_ref):
    tm = h2_ref.shape[0]
    nt = (((1,), (1,)), ((), ()))
    ninf = -3.4e38
    qt = lax.dot_general(wqt_ref[...], h2_ref[...].astype(BF16), nt, preferred_element_type=F32)
    k = P_TOPK
    rowk = lax.broadcasted_iota(jnp.int32, (k, tm), 0)

    def top_rows(sc, payload):
        nrow = sc.shape[0]
        rows = lax.broadcasted_iota(jnp.int32, (nrow, tm), 0)

        def body(r, carry):
            sc, vals, pay = carry
            mx = jnp.max(sc, axis=0, keepdims=True)
            idx = jnp.min(jnp.where(sc == mx, rows, nrow), axis=0, keepdims=True)
            pick = rows == idx
            got = idx if payload is None else jnp.sum(jnp.where(pick, payload, 0), axis=0, keepdims=True)
            vals = jnp.where(rowk == r, mx, vals)
            pay = jnp.where(rowk == r, got, pay)
            return jnp.where(pick, ninf, sc), vals, pay

        _, vals, pay = lax.fori_loop(0, k, body, (sc, jnp.zeros((k, tm), F32), jnp.zeros((k, tm), jnp.int32)))
        return vals, pay

    s1, i1 = top_rows(jnp.dot(sk_ref[0], qt[0:P_HALF], preferred_element_type=F32, precision=HIGHEST), None)
    s2, i2 = top_rows(jnp.dot(sk_ref[1], qt[P_HALF:], preferred_element_type=F32, precision=HIGHEST), None)
    cand = jnp.concatenate([s1[a:a + 1] + s2 for a in range(k)], axis=0)
    cidx = jnp.concatenate([i1[a:a + 1] * P_NKEYS + i2 for a in range(k)], axis=0)
    top, e = top_rows(cand, cidx)
    ex = jnp.exp(top - top[0:1])
    eidx_ref[0] = e
    gate_ref[0] = ex / jnp.sum(ex, axis=0, keepdims=True)


def _peer_route(h2, wq, subkeys, seq):
    n, d = h2.shape
    tm = min(256, seq)
    wqt = wq.T.astype(BF16)
    out = pl.BlockSpec((1, P_TOPK, tm), lambda i, h: (h, 0, i))
    return pl.pallas_call(
        _peer_route_kernel,
        grid=(n // tm, P_HEADS),
        in_specs=[pl.BlockSpec((tm, d), lambda i, h: (i, 0)),
                  pl.BlockSpec((P_QDIM, d), lambda i, h: (h, 0)),
                  pl.BlockSpec((2, P_NKEYS, P_HALF), lambda i, h: (0, 0, 0))],
        out_specs=[out, out],
        out_shape=[jax.ShapeDtypeStruct((P_HEADS, P_TOPK, n), jnp.int32),
                   jax.ShapeDtypeStruct((P_HEADS, P_TOPK, n), F32)],
        compiler_params=_cparams("parallel", "arbitrary"),
        name="peer_route",
    )(h2, wqt, subkeys)


PEER_TB = 8
PEER_SEL = P_HEADS * P_TOPK


def _peer_expert_kernel(eidx_ref, enext_ref, gt_ref, h2_ref, x1_ref, g2_ref, tab_ref, o_ref, buf, sem):
    i = pl.program_id(0)
    n_steps = pl.num_programs(0)
    slot = i % 2
    rows = PEER_TB * PEER_SEL

    def issue(idx_ref, dst_slot):
        def per_token(t, carry):
            for j in range(PEER_SEL):
                e = idx_ref[t, j]
                pltpu.make_async_copy(tab_ref.at[pl.ds(e, 1), :],
                                      buf.at[dst_slot, pl.ds(t * PEER_SEL + j, 1), :],
                                      sem.at[dst_slot]).start()
            return carry
        lax.fori_loop(0, PEER_TB, per_token, 0)

    @pl.when(i == 0)
    def _():
        issue(eidx_ref, 0)

    @pl.when(i + 1 < n_steps)
    def _():
        issue(enext_ref, 1 - slot)

    pltpu.make_async_copy(tab_ref.at[pl.ds(0, rows), :], buf.at[slot], sem.at[slot]).wait()

    g2 = g2_ref[0]
    gt = gt_ref[0]
    for t in range(PEER_TB):
        w = buf[slot, t * PEER_SEL:(t + 1) * PEER_SEL, :]
        u = pltpu.bitcast(w & jnp.uint32(0xFFFF0000), F32)
        v = pltpu.bitcast(w << 16, F32)
        xr = h2_ref[t:t + 1, :]
        a = jnp.sum(u * xr, axis=1, keepdims=True)
        wg = gt[:, t:t + 1] * _gelu_tanh(a)
        y = jnp.sum(v * wg, axis=0, keepdims=True)
        o_ref[t:t + 1, :] = x1_ref[t:t + 1, :] + g2 * y


def _pack_expert_table(u_tab, v_tab):
    ub = lax.bitcast_convert_type(u_tab.astype(BF16), jnp.uint16).astype(jnp.uint32)
    vb = lax.bitcast_convert_type(v_tab.astype(BF16), jnp.uint16).astype(jnp.uint32)
    return (ub << 16) | vb


def _peer_experts(eidx_t, gate_t, h2, x1, gate2, table, seq):
    n, d = h2.shape
    tb = PEER_TB
    steps = n // tb
    eidx = eidx_t.reshape(PEER_SEL, n).T
    gate_t = gate_t.reshape(PEER_SEL, steps, tb).transpose(1, 0, 2)
    row = pl.BlockSpec((tb, d), lambda i: (i, 0))
    return pl.pallas_call(
        _peer_expert_kernel,
        grid=(steps,),
        in_specs=[pl.BlockSpec((tb, PEER_SEL), lambda i: (i, 0), memory_space=pltpu.SMEM),
                  pl.BlockSpec((tb, PEER_SEL), lambda i: (jnp.minimum(i + 1, steps - 1), 0),
                               memory_space=pltpu.SMEM),
                  pl.BlockSpec((1, PEER_SEL, tb), lambda i: (i, 0, 0)),
                  row, row,
                  pl.BlockSpec((1, 1, d), lambda i: ((i * tb) // seq, 0, 0)),
                  pl.BlockSpec(memory_space=pl.ANY)],
        out_specs=row,
        out_shape=jax.ShapeDtypeStruct((n, d), F32),
        scratch_shapes=[pltpu.VMEM((2, tb * PEER_SEL, d), jnp.uint32), pltpu.SemaphoreType.DMA((2,))],
        compiler_params=_cparams("arbitrary"),
        name="peer_experts",
    )(eidx, eidx, gate_t, h2, x1, gate2, table)


def kernel(x, c, w_mod, b_mod, norm1_g, norm2_g, w_in, conv_qk, b_igate, b_fgate, mlstm_norm_g, qn_g, kn_g,
           cmp_pos_k, cmp_pos_v, cmp_k_w1, cmp_k_w2, cmp_v_w1, cmp_v_w2, w_out, peer_wq, peer_subkeys,
           peer_u, peer_v):
    b_, s_, d = x.shape
    n = b_ * s_
    for l in range(w_mod.shape[0]):
        mod = _modulation(c, w_mod[l], b_mod[l]).reshape(b_, 6, 1, d)
        shift1, scale1, gate1, shift2, scale2, gate2 = (mod[:, i] for i in range(6))
        x2 = x.reshape(n, d)
        z2 = _in_projection(x2, norm1_g[l], scale1, shift1, _pad_in_weights(w_in[l]), s_)
        z3 = z2.reshape(b_, s_, Z_COLS)
        gate_bias = jnp.concatenate([b_igate[l], b_fgate[l], jnp.zeros((LANES - 2 * M_HEADS,), F32)])
        hm = _mlstm(z3, conv_qk[l], gate_bias.reshape(1, LANES), mlstm_norm_g[l])
        qn, kc, vc, ks, vs, kw, vw = _nsa_prep(z3, qn_g[l], kn_g[l])
        k_cmp = _compress(kc, cmp_pos_k[l], cmp_k_w1[l], cmp_k_w2[l], kn_g[l, 0], True)
        v_cmp = _compress(vc, cmp_pos_v[l], cmp_v_w1[l], cmp_v_w2[l], kn_g[l, 0], False)
        o_cmp, sel = _nsa_cmp(qn, k_cmp, v_cmp)
        o_win, o_sel = _nsa_attn(qn, kw, vw, ks, vs, sel)
        x1, h2 = _out_projection(hm.reshape(n, M_WIDTH), o_cmp.reshape(n, A_WIDTH), o_sel.reshape(n, A_WIDTH),
                                 o_win.reshape(n, A_WIDTH), z2, x2, w_out[l], gate1, norm2_g[l], scale2, shift2, s_)
        eidx, gate = _peer_route(h2, peer_wq[l], peer_subkeys[l], s_)
        x = _peer_experts(eidx, gate, h2, x1, gate2, _pack_expert_table(peer_u[l], peer_v[l]), s_).reshape(b_, s_, d)
    return x
```

```python
import functools

import numpy as np
import jax
import jax.numpy as jnp
from jax import lax
from jax.experimental import pallas as pl
from jax.experimental.pallas import tpu as pltpu

F32 = jnp.float32
BF16 = jnp.bfloat16
HIGHEST = lax.Precision.HIGHEST

EPS = 1e-6
D_MODEL = 2048
M_HEADS = 4
M_HEAD_DIM = 256
M_WIDTH = M_HEADS * M_HEAD_DIM
M_CHUNK = 256
CONV_W = 4
A_GROUPS = 4
A_HPG = 4
A_HEADS = A_GROUPS * A_HPG
A_HEAD_DIM = 64
A_WIDTH = A_HEADS * A_HEAD_DIM
CMP_BLOCK = 32
CMP_STRIDE = 16
CMP_HIDDEN = 256
SEL_BLOCK = 64
SEL_TOP = 16
WINDOW = 512
Q_BLOCK = 128
FORCE_SCORE = 1e4
P_HEADS = 8
P_NKEYS = 128
P_TOPK = 16
P_QDIM = 256
P_HALF = 128
MV_OFF = 2 * M_WIDTH
MO_OFF = 3 * M_WIDTH
MIF_OFF = 4 * M_WIDTH
AQ_OFF = MIF_OFF + 2 * M_HEADS
AKV_OFF = AQ_OFF + A_WIDTH
AG_OFF = AKV_OFF + 6 * A_GROUPS * A_HEAD_DIM
IN_COLS = AG_OFF + 3 * A_HEADS
Z_M = 0
Z_AQ = 4096
Z_KV = 5120
Z_IF = 6656
Z_AG = 6784
Z_COLS = 6912
LANES = 128
NEG = -1e30
VMEM_LIMIT = 56 * 1024 * 1024


def _cparams(*sem):
    return pltpu.CompilerParams(dimension_semantics=sem, vmem_limit_bytes=VMEM_LIMIT)


def _gelu_tanh(x):
    return 0.5 * x * (1.0 + jnp.tanh(np.sqrt(2.0 / np.pi).astype(np.float32) * (x + 0.044715 * (x * x * x))))


def _mod_kernel(c_ref, w_ref, b_ref, o_ref):
    c = c_ref[...]
    s = c * jax.nn.sigmoid(c)
    o_ref[...] = jnp.dot(s, w_ref[...], preferred_element_type=F32, precision=HIGHEST) + b_ref[...]


def _modulation(c, w_mod, b_mod):
    b_, d = c.shape
    n = w_mod.shape[1]
    tn = 1024
    cp = jnp.pad(c, ((0, 8 - b_), (0, 0)))
    out = pl.pallas_call(
        _mod_kernel,
        grid=(n // tn,),
        in_specs=[pl.BlockSpec((8, d), lambda j: (0, 0)),
                  pl.BlockSpec((d, tn), lambda j: (0, j)),
                  pl.BlockSpec((1, tn), lambda j: (0, j))],
        out_specs=pl.BlockSpec((8, tn), lambda j: (0, j)),
        out_shape=jax.ShapeDtypeStruct((8, n), F32),
        compiler_params=_cparams("parallel"),
        name="modulation",
    )(cp, w_mod, b_mod.reshape(1, n))
    return out[:b_]


def _inproj_kernel(x_ref, g_ref, sc_ref, sh_ref, w_ref, o_ref, h_scr):
    @pl.when(pl.program_id(1) == 0)
    def _():
        x = x_ref[...]
        r = x * lax.rsqrt(jnp.mean(x * x, axis=-1, keepdims=True) + EPS)
        h = (r * g_ref[...]) * (1.0 + sc_ref[0]) + sh_ref[0]
        h_scr[...] = h.astype(BF16)

    o_ref[...] = jnp.dot(h_scr[...], w_ref[...], preferred_element_type=F32)


def _in_projection(x2, g1, scale1, shift1, w_pad, seq):
    n, d = x2.shape
    zc = w_pad.shape[1]
    tm = min(512, seq)
    tn = 768
    return pl.pallas_call(
        _inproj_kernel,
        grid=(n // tm, zc // tn),
        in_specs=[pl.BlockSpec((tm, d), lambda i, j: (i, 0)),
                  pl.BlockSpec((1, d), lambda i, j: (0, 0)),
                  pl.BlockSpec((1, 1, d), lambda i, j: ((i * tm) // seq, 0, 0)),
                  pl.BlockSpec((1, 1, d), lambda i, j: ((i * tm) // seq, 0, 0)),
                  pl.BlockSpec((d, tn), lambda i, j: (0, j))],
        out_specs=pl.BlockSpec((tm, tn), lambda i, j: (i, j)),
        out_shape=jax.ShapeDtypeStruct((n, zc), F32),
        scratch_shapes=[pltpu.VMEM((tm, d), BF16)],
        compiler_params=_cparams("parallel", "arbitrary"),
        name="in_projection",
    )(x2, g1.reshape(1, d), scale1, shift1, w_pad)


def _pad_in_weights(w_in):
    d = w_in.shape[0]
    z = lambda k: jnp.zeros((d, k), w_in.dtype)
    w = jnp.concatenate([
        w_in[:, :MIF_OFF],
        w_in[:, AQ_OFF:AKV_OFF],
        w_in[:, AKV_OFF:AG_OFF],
        w_in[:, MIF_OFF:AQ_OFF], z(LANES - 2 * M_HEADS),
        w_in[:, AG_OFF:IN_COLS], z(LANES - 3 * A_HEADS),
    ], axis=1)
    return w.astype(BF16)


def _mlstm_kernel(q_ref, k_ref, v_ref, o_ref, if_ref, cq_ref, ck_ref, bias_ref, g_ref, out_ref,
                  c_scr, n_scr, m_scr, qt_scr, kt_scr):
    L, d = q_ref.shape[1], q_ref.shape[2]
    h = pl.program_id(1)

    @pl.when(pl.program_id(2) == 0)
    def _():
        c_scr[...] = jnp.zeros_like(c_scr)
        n_scr[...] = jnp.zeros_like(n_scr)
        m_scr[...] = jnp.zeros_like(m_scr)
        qt_scr[...] = jnp.zeros_like(qt_scr)
        kt_scr[...] = jnp.zeros_like(kt_scr)

    def conv_silu(raw, tail_scr, w_ref):
        ext = jnp.concatenate([tail_scr[...], raw], axis=0)
        acc = jnp.zeros((L, d), F32)
        for j in range(CONV_W):
            off = 8 - (CONV_W - 1) + j
            acc = acc + w_ref[j:j + 1, :] * ext[off:off + L, :]
        tail_scr[...] = raw[L - 8:, :]
        return acc * jax.nn.sigmoid(acc)

    qc = conv_silu(q_ref[0], qt_scr, cq_ref)
    kc = conv_silu(k_ref[0], kt_scr, ck_ref) * (d ** -0.5)
    vc = v_ref[0]

    gt = if_ref[0] + bias_ref[...]
    lane = lax.broadcasted_iota(jnp.int32, (L, LANES), 1)
    logsig = jnp.minimum(gt, 0.0) - jnp.log(1.0 + jnp.exp(-jnp.abs(gt)))
    gt = jnp.where(lane < M_HEADS, gt, logsig)
    rows = lax.broadcasted_iota(jnp.int32, (L, L), 0)
    cols = lax.broadcasted_iota(jnp.int32, (L, L), 1)
    causal = rows >= cols
    tril = jnp.where(causal, 1.0, 0.0).astype(F32)
    csum = jnp.dot(tril, gt, preferred_element_type=F32, precision=HIGHEST)
    gt_t = gt.T
    csum_t = csum.T
    sub = lax.broadcasted_iota(jnp.int32, (LANES, L), 0)
    ig_col = jnp.sum(jnp.where(lane == h, gt, 0.0), axis=1, keepdims=True)
    b_col = jnp.sum(jnp.where(lane == h + M_HEADS, csum, 0.0), axis=1, keepdims=True)
    ig_row = jnp.sum(jnp.where(sub == h, gt_t, 0.0), axis=0, keepdims=True)
    b_row = jnp.sum(jnp.where(sub == h + M_HEADS, csum_t, 0.0), axis=0, keepdims=True)

    m_st = m_scr[0:1, 0:1]
    dmat = jnp.where(causal, b_col - b_row + ig_row, NEG)
    m_inter = b_col + m_st
    m_t = jnp.maximum(m_inter, jnp.max(dmat, axis=1, keepdims=True))
    qb = qc.astype(BF16)
    kb = kc.astype(BF16)
    vb = vc.astype(BF16)
    qk = lax.dot_general(qb, kb, (((1,), (1,)), ((), ())), preferred_element_type=F32)
    w = jnp.exp(dmat - m_t) * qk
    a_inter = jnp.exp(m_inter - m_t)
    c_st = c_scr[...]
    n_st = n_scr[...]
    num = (jnp.dot(w.astype(BF16), vb, preferred_element_type=F32)
           + a_inter * jnp.dot(qb, c_st.astype(BF16), preferred_element_type=F32))
    den = jnp.sum(w, axis=1, keepdims=True) + a_inter * jnp.sum(qc * n_st, axis=1, keepdims=True)
    hout = num / jnp.maximum(jnp.abs(den), jnp.exp(-m_t))

    b_last = b_col[L - 1:L, :]
    a_s = b_last - b_col + ig_col
    m_new = jnp.maximum(b_last + m_st, jnp.max(a_s, axis=0, keepdims=True))
    w_s = jnp.exp(a_s - m_new)
    decay = jnp.exp(b_last + m_st - m_new)
    kw = kc * w_s
    c_scr[...] = decay * c_st + lax.dot_general(kw.astype(BF16), vb, (((0,), (0,)), ((), ())),
                                                preferred_element_type=F32)
    n_scr[...] = decay * n_st + jnp.sum(kw, axis=0, keepdims=True)
    m_scr[...] = jnp.broadcast_to(m_new, m_scr.shape)

    r = hout * lax.rsqrt(jnp.mean(hout * hout, axis=-1, keepdims=True) + EPS)
    out_ref[0] = (r * g_ref[...]) * jax.nn.sigmoid(o_ref[0])


def _mlstm(z3, conv_qk, gate_bias, norm_g):
    b_, s_, _ = z3.shape
    L = min(M_CHUNK, s_)
    d = M_HEAD_DIM
    H = M_HEADS
    blk = lambda off: pl.BlockSpec((1, L, d), lambda b, h, c: (b, c, off + h))
    return pl.pallas_call(
        _mlstm_kernel,
        grid=(b_, H, s_ // L),
        in_specs=[blk(0), blk(H), blk(2 * H), blk(3 * H),
                  pl.BlockSpec((1, L, LANES), lambda b, h, c: (b, c, Z_IF // LANES)),
                  pl.BlockSpec((CONV_W, d), lambda b, h, c: (0, h)),
                  pl.BlockSpec((CONV_W, d), lambda b, h, c: (0, H + h)),
                  pl.BlockSpec((1, LANES), lambda b, h, c: (0, 0)),
                  pl.BlockSpec((1, d), lambda b, h, c: (0, h))],
        out_specs=pl.BlockSpec((1, L, d), lambda b, h, c: (b, c, h)),
        out_shape=jax.ShapeDtypeStruct((b_, s_, M_WIDTH), F32),
        scratch_shapes=[pltpu.VMEM((d, d), F32), pltpu.VMEM((1, d), F32), pltpu.VMEM((8, LANES), F32),
                        pltpu.VMEM((8, d), F32), pltpu.VMEM((8, d), F32)],
        compiler_params=_cparams("parallel", "parallel", "arbitrary"),
        name="mlstm",
    )(z3, z3, z3, z3, z3, conv_qk, conv_qk, gate_bias, norm_g.reshape(1, M_WIDTH))


SEL_TILE = 512
LOG2E = float(np.log2(np.e))


def _nsa_prep_kernel(aq_ref, kc_ref, vc_ref, ks_ref, vs_ref, kw_ref, vw_ref, qg_ref, kg_ref, slc_ref,
                     qn_o, qa_o, kc_o, vc_o, ks_o, vs_o, kw_o, vw_o):
    hd = A_HEAD_DIM
    ts = aq_ref.shape[1]

    def norm(xh, g):
        return (xh * lax.rsqrt(jnp.mean(xh * xh, axis=-1, keepdims=True) + EPS)) * g

    aq = aq_ref[0]
    qg = qg_ref[...]
    for h in range(A_HEADS):
        qh = norm(aq[:, h * hd:(h + 1) * hd], qg) * (hd ** -0.5)
        qn_o[0, h // A_HPG, h % A_HPG] = qh.astype(BF16)
        slope_cols = jnp.broadcast_to(slc_ref[h:h + 1, :], (ts, hd))
        qa_o[0, h // A_HPG, h % A_HPG] = jnp.concatenate([qh * LOG2E, slope_cols], axis=1).astype(BF16)
    pos = pl.program_id(1) * ts + lax.broadcasted_iota(jnp.int32, (ts, hd), 0)
    lane = lax.broadcasted_iota(jnp.int32, (ts, hd), 1)
    c = pos & (SEL_TILE - 1)
    kind = lambda r: (lane == r) | (lane == r + 3) | (lane == r + 6)
    pos_cols = jnp.where(kind(0), c & 255, jnp.where(kind(1), c & 256, jnp.where(kind(2), pos - c, 0))).astype(F32)
    kc, vc, ks, vs, kw, vw = kc_ref[0], vc_ref[0], ks_ref[0], vs_ref[0], kw_ref[0], vw_ref[0]
    for g in range(A_GROUPS):
        sl = slice(g * hd, (g + 1) * hd)
        kc_o[0, g] = kc[:, sl]
        vc_o[0, g] = vc[:, sl]
        ks_o[0, g] = jnp.concatenate([norm(ks[:, sl], kg_ref[1:2, :]), pos_cols], axis=1).astype(BF16)
        vs_o[0, g] = vs[:, sl].astype(BF16)
        kw_o[0, g] = norm(kw[:, sl], kg_ref[2:3, :]).astype(BF16)
        vw_o[0, g] = vw[:, sl].astype(BF16)


def _slope_cols_table():
    slopes = 2.0 ** (-8.0 * (np.arange(A_HEADS) + 1) / A_HEADS) * np.log2(np.e)
    tab = np.zeros((A_HEADS, A_HEAD_DIM), np.float32)
    rem = slopes.astype(np.float64)
    for i in range(3):
        part = rem.astype(np.float32).astype(BF16).astype(np.float64)
        tab[:, 3 * i:3 * i + 3] = part[:, None]
        rem = rem - part
    return jnp.asarray(tab)


def _nsa_prep(z3, qn_g, kn_g):
    b_, s_, _ = z3.shape
    ts = min(512, s_)
    G, hd = A_GROUPS, A_HEAD_DIM
    kvw = G * hd
    kv_in = lambda i: pl.BlockSpec((1, ts, kvw), lambda b, t: (b, t, Z_KV // kvw + i))
    kv_out = lambda w: pl.BlockSpec((1, G, ts, w), lambda b, t: (b, 0, t, 0))
    q_out = lambda w: pl.BlockSpec((1, G, A_HPG, ts, w), lambda b, t: (b, 0, 0, t, 0))
    kv_shape = lambda dt, w=hd: jax.ShapeDtypeStruct((b_, G, s_, w), dt)
    return pl.pallas_call(
        _nsa_prep_kernel,
        grid=(b_, s_ // ts),
        in_specs=[pl.BlockSpec((1, ts, A_WIDTH), lambda b, t: (b, t, Z_AQ // A_WIDTH))]
                 + [kv_in(i) for i in range(6)]
                 + [pl.BlockSpec((1, hd), lambda b, t: (0, 0)), pl.BlockSpec((3, hd), lambda b, t: (0, 0)),
                    pl.BlockSpec((A_HEADS, hd), lambda b, t: (0, 0))],
        out_specs=[q_out(hd), q_out(2 * hd), kv_out(hd), kv_out(hd), kv_out(2 * hd), kv_out(hd), kv_out(hd),
                   kv_out(hd)],
        out_shape=[jax.ShapeDtypeStruct((b_, G, A_HPG, s_, hd), BF16),
                   jax.ShapeDtypeStruct((b_, G, A_HPG, s_, 2 * hd), BF16),
                   kv_shape(F32), kv_shape(F32), kv_shape(BF16, 2 * hd), kv_shape(BF16), kv_shape(BF16),
                   kv_shape(BF16)],
        compiler_params=_cparams("parallel", "parallel"),
        name="nsa_prep",
    )(z3, z3, z3, z3, z3, z3, z3, qn_g.reshape(1, hd), kn_g, _slope_cols_table())


def _compress_kernel(r_ref, pos_ref, w1a_ref, w1b_ref, w2_ref, g_ref, o_ref, *, do_norm):
    r = r_ref[0, 0]
    nr = r.shape[0]
    u = jnp.dot((r + pos_ref[0:1, :]).astype(BF16), w1a_ref[...], preferred_element_type=F32)
    v = jnp.dot((r + pos_ref[1:2, :]).astype(BF16), w1b_ref[...], preferred_element_type=F32)
    pre = u + pltpu.roll(v, nr - 1, 0)
    out = jnp.dot(_gelu_tanh(pre).astype(BF16), w2_ref[...], preferred_element_type=F32)
    if do_norm:
        out = (out * lax.rsqrt(jnp.mean(out * out, axis=-1, keepdims=True) + EPS)) * g_ref[...]
    o_ref[0, 0] = out.astype(BF16)


def _compress(a, pos, w1, w2, g, do_norm):
    b_, G, s_, hd = a.shape
    nr = s_ // CMP_STRIDE
    half = CMP_STRIDE * hd
    r = a.reshape(b_, G, nr, half)
    w1b16 = w1.astype(BF16)
    return pl.pallas_call(
        functools.partial(_compress_kernel, do_norm=do_norm),
        grid=(b_, G),
        in_specs=[pl.BlockSpec((1, 1, nr, half), lambda b, g: (b, g, 0, 0)),
                  pl.BlockSpec((2, half), lambda b, g: (0, 0)),
                  pl.BlockSpec((half, CMP_HIDDEN), lambda b, g: (0, 0)),
                  pl.BlockSpec((half, CMP_HIDDEN), lambda b, g: (1, 0)),
                  pl.BlockSpec((CMP_HIDDEN, hd), lambda b, g: (0, 0)),
                  pl.BlockSpec((1, hd), lambda b, g: (0, 0))],
        out_specs=pl.BlockSpec((1, 1, nr, hd), lambda b, g: (b, g, 0, 0)),
        out_shape=jax.ShapeDtypeStruct((b_, G, nr, hd), BF16),
        compiler_params=_cparams("parallel", "parallel"),
        name="nsa_compress_norm" if do_norm else "nsa_compress",
    )(r, pos.reshape(2, half), w1b16, w1b16, w2.astype(BF16), g.reshape(1, hd))


def _nsa_cmp_kernel(q_ref, kc_ref, vc_ref, ov_ref, slope_ref, ocmp_ref, sel_ref, *, n_cmp, n_blk, n_sel):
    T, hd = q_ref.shape[3], q_ref.shape[4]
    R = A_HPG * T
    nc = kc_ref.shape[2]
    q0 = pl.program_id(2) * T
    q = q_ref[0, 0].reshape(R, hd)
    s = lax.dot_general(q, kc_ref[0, 0], (((1,), (1,)), ((), ())), preferred_element_type=F32)
    row = lax.broadcasted_iota(jnp.int32, (R, nc), 0)
    ci = lax.broadcasted_iota(jnp.int32, (R, nc), 1)
    t = q0 + (row & (T - 1))
    disti = t - (ci * CMP_STRIDE + (CMP_BLOCK - 1))
    valid = (disti >= 0) & (ci < n_cmp)
    slope = slope_ref[0][:, 0:1]
    s = jnp.where(valid, s - slope * disti.astype(F32), NEG)
    m = jnp.max(s, axis=1, keepdims=True)
    e = jnp.where(valid, jnp.exp(s - m), 0.0)
    p = e / jnp.maximum(jnp.sum(e, axis=1, keepdims=True), 1e-30)
    oc = jnp.dot(p.astype(BF16), vc_ref[0, 0], preferred_element_type=F32)
    for h in range(A_HPG):
        ocmp_ref[0, :, h * hd:(h + 1) * hd] = oc[h * T:(h + 1) * T, :]
    ps = p[0:T] + p[T:2 * T] + p[2 * T:3 * T] + p[3 * T:4 * T]
    hi = ps.astype(BF16)
    lo = (ps - hi.astype(F32)).astype(BF16)
    ov = ov_ref[...]
    imp = jnp.dot(hi, ov, preferred_element_type=F32) + jnp.dot(lo, ov, preferred_element_type=F32)
    imp = imp.T
    blk = lax.broadcasted_iota(jnp.int32, (LANES, T), 0)
    cur = (q0 + lax.broadcasted_iota(jnp.int32, (LANES, T), 1)) // SEL_BLOCK
    forced = (blk == 0) | (blk == cur) | (blk == cur - 1)
    imp = jnp.where(forced, FORCE_SCORE, jnp.where(blk <= cur, imp, -FORCE_SCORE))
    imp = jnp.where(blk < n_blk, imp, -3e38)
    sel = jnp.zeros((LANES, T), F32)
    for _ in range(n_sel):
        mx = jnp.max(imp, axis=0, keepdims=True)
        idx = jnp.min(jnp.where(imp == mx, blk, LANES), axis=0, keepdims=True)
        pick = blk == idx
        sel = jnp.where(pick, 1.0, sel)
        imp = jnp.where(pick, -3.4e38, imp)
    sel_ref[0, 0] = sel.T.astype(BF16)


def _slope_table():
    slopes = (2.0 ** (-8.0 * (np.arange(A_HEADS) + 1) / A_HEADS)).astype(np.float32).reshape(A_GROUPS, A_HPG)
    tab = np.repeat(slopes, Q_BLOCK, axis=1)
    return jnp.asarray(np.broadcast_to(tab[:, :, None], (A_GROUPS, A_HPG * Q_BLOCK, LANES)).copy())


def _overlap_table(s_, nc):
    n_cmp = (s_ - CMP_BLOCK) // CMP_STRIDE + 1
    n_blk = s_ // SEL_BLOCK
    starts = np.arange(nc)[:, None] * CMP_STRIDE
    blk_starts = np.arange(LANES)[None, :] * SEL_BLOCK
    ov = (starts < blk_starts + SEL_BLOCK) & (starts + CMP_BLOCK > blk_starts)
    ov &= (np.arange(nc)[:, None] < n_cmp) & (np.arange(LANES)[None, :] < n_blk)
    return jnp.asarray(ov.astype(np.float32)).astype(BF16)


def _nsa_cmp(qn, k_cmp, v_cmp):
    b_, G, hpg, s_, hd = qn.shape
    T = Q_BLOCK
    nc = k_cmp.shape[2]
    n_cmp = (s_ - CMP_BLOCK) // CMP_STRIDE + 1
    n_blk = s_ // SEL_BLOCK
    kern = functools.partial(_nsa_cmp_kernel, n_cmp=n_cmp, n_blk=n_blk, n_sel=min(SEL_TOP, n_blk))
    return pl.pallas_call(
        kern,
        grid=(b_, G, s_ // T),
        in_specs=[pl.BlockSpec((1, 1, hpg, T, hd), lambda b, g, i: (b, g, 0, i, 0)),
                  pl.BlockSpec((1, 1, nc, hd), lambda b, g, i: (b, g, 0, 0)),
                  pl.BlockSpec((1, 1, nc, hd), lambda b, g, i: (b, g, 0, 0)),
                  pl.BlockSpec((nc, LANES), lambda b, g, i: (0, 0)),
                  pl.BlockSpec((1, hpg * T, LANES), lambda b, g, i: (g, 0, 0))],
        out_specs=[pl.BlockSpec((1, T, hpg * hd), lambda b, g, i: (b, i, g)),
                   pl.BlockSpec((1, 1, T, LANES), lambda b, g, i: (b, g, i, 0))],
        out_shape=[jax.ShapeDtypeStruct((b_, s_, A_WIDTH), F32),
                   jax.ShapeDtypeStruct((b_, G, s_, LANES), BF16)],
        compiler_params=_cparams("parallel", "parallel", "parallel"),
        name="nsa_cmp_topk",
    )(qn, k_cmp, v_cmp, _overlap_table(s_, nc), _slope_table())


def _nsa_attn_kernel(q_ref, qa_ref, kw_ref, vw_ref, ks_ref, vs_ref, sel_ref, slope_ref, owin_ref, osel_ref,
                     m_scr, l_scr, acc_scr, *, wk, tk):
    T, hd = q_ref.shape[3], q_ref.shape[4]
    R = A_HPG * T
    qb = pl.program_id(2)
    q0 = qb * T
    q = q_ref[0, 0].reshape(R, hd)
    slope = slope_ref[0][:, 0:1]
    nt = (((1,), (1,)), ((), ()))

    start = pl.multiple_of(jnp.maximum(q0 + T - wk, 0), T)
    kw = kw_ref[0, 0, pl.ds(start, wk), :]
    vw = vw_ref[0, 0, pl.ds(start, wk), :]
    s = lax.dot_general(q, kw, nt, preferred_element_type=F32)
    t = q0 + (lax.broadcasted_iota(jnp.int32, (R, wk), 0) & (T - 1))
    disti = t - (start + lax.broadcasted_iota(jnp.int32, (R, wk), 1))
    mask = (disti >= 0) & (disti < WINDOW)
    s = jnp.where(mask, s - slope * disti.astype(F32), NEG)
    m = jnp.max(s, axis=1, keepdims=True)
    e = jnp.exp(s - m)
    ow = jnp.dot(e.astype(BF16), vw, preferred_element_type=F32) / jnp.sum(e, axis=1, keepdims=True)
    for h in range(A_HPG):
        owin_ref[0, :, h * hd:(h + 1) * hd] = ow[h * T:(h + 1) * T, :]

    qa = qa_ref[0, 0].reshape(R, 2 * hd)
    sel = sel_ref[0, 0]
    sel_bias = ((sel.astype(F32) - 1.0) * 1e30).astype(BF16)
    m_scr[...] = jnp.full_like(m_scr, NEG)
    l_scr[...] = jnp.zeros_like(l_scr)
    acc_scr[...] = jnp.zeros_like(acc_scr)
    bpt = tk // SEL_BLOCK
    lane = lax.broadcasted_iota(jnp.int32, (T, LANES), 1)
    j_diag = q0 // tk

    def tile(j, diagonal):
        kv0 = pl.multiple_of(j * tk, tk)
        bi = lax.broadcasted_iota(jnp.int32, (LANES, tk), 0)
        cc = lax.broadcasted_iota(jnp.int32, (LANES, tk), 1)
        expand = jnp.where(((kv0 + cc) // SEL_BLOCK) == bi, 1.0, 0.0).astype(BF16)
        bias = jnp.dot(sel_bias, expand, preferred_element_type=F32)
        k = ks_ref[0, 0, pl.ds(kv0, tk), :]
        v = vs_ref[0, 0, pl.ds(kv0, tk), :]
        sc = lax.dot_general(qa, k, nt, preferred_element_type=F32) + jnp.concatenate([bias] * A_HPG, axis=0)
        if diagonal:
            tt = q0 + (lax.broadcasted_iota(jnp.int32, (R, tk), 0) & (T - 1))
            sc = jnp.where(tt >= kv0 + lax.broadcasted_iota(jnp.int32, (R, tk), 1), sc, NEG)
        m_old = m_scr[...]
        m_new = jnp.maximum(m_old, jnp.max(sc, axis=1, keepdims=True))
        alpha = jnp.exp2(m_old - m_new)
        p = jnp.exp2(sc - m_new)
        l_scr[...] = alpha * l_scr[...] + jnp.sum(p, axis=1, keepdims=True)
        acc_scr[...] = alpha * acc_scr[...] + jnp.dot(p.astype(BF16), v, preferred_element_type=F32)
        m_scr[...] = m_new

    def body(j, carry):
        in_tile = (lane >= j * bpt) & (lane < (j + 1) * bpt)
        has = jnp.max(jnp.where(in_tile, sel.astype(F32), 0.0)) > 0.5

        @pl.when(jnp.logical_or(has, j == 0))
        def _():
            tile(j, False)

        return carry

    lax.fori_loop(0, j_diag, body, 0)
    tile(j_diag, True)
    os_ = acc_scr[...] / l_scr[...]
    for h in range(A_HPG):
        osel_ref[0, :, h * hd:(h + 1) * hd] = os_[h * T:(h + 1) * T, :]


def _nsa_attn(qn, qa, k_win, v_win, k_sel_aug, v_sel, sel):
    b_, G, hpg, s_, hd = qn.shape
    T = Q_BLOCK
    wk = min(WINDOW + T, s_)
    tk = SEL_TILE
    assert s_ % tk == 0
    full = lambda w: pl.BlockSpec((1, 1, s_, w), lambda b, g, i: (b, g, 0, 0))
    q_blk = lambda w: pl.BlockSpec((1, 1, hpg, T, w), lambda b, g, i: (b, g, 0, i, 0))
    out = pl.BlockSpec((1, T, hpg * hd), lambda b, g, i: (b, i, g))
    return pl.pallas_call(
        functools.partial(_nsa_attn_kernel, wk=wk, tk=tk),
        grid=(b_, G, s_ // T),
        in_specs=[q_blk(hd), q_blk(2 * hd),
                  full(hd), full(hd), full(2 * hd), full(hd),
                  pl.BlockSpec((1, 1, T, LANES), lambda b, g, i: (b, g, i, 0)),
                  pl.BlockSpec((1, hpg * T, LANES), lambda b, g, i: (g, 0, 0))],
        out_specs=[out, out],
        out_shape=[jax.ShapeDtypeStruct((b_, s_, A_WIDTH), F32)] * 2,
        scratch_shapes=[pltpu.VMEM((hpg * T, 1), F32), pltpu.VMEM((hpg * T, 1), F32),
                        pltpu.VMEM((hpg * T, hd), F32)],
        compiler_params=_cparams("parallel", "parallel", "arbitrary"),
        name="nsa_window_selected",
    )(qn, qa, k_win, v_win, k_sel_aug, v_sel, sel, _slope_table())


def _outproj_kernel(hm_ref, oc_ref, os_ref, ow_ref, ag_ref, ex_ref, x_ref, wo_ref, g1_ref, n2_ref, sc2_ref,
                    sh2_ref, x1_ref, h2_ref):
    sg = jax.nn.sigmoid(ag_ref[...])
    hi = sg.astype(BF16)
    lo = (sg - hi.astype(F32)).astype(BF16)
    ha = jnp.zeros(oc_ref.shape, F32)
    for br, o_ref in enumerate((oc_ref, os_ref, ow_ref)):
        ex = ex_ref[br]
        gexp = jnp.dot(hi, ex, preferred_element_type=F32) + jnp.dot(lo, ex, preferred_element_type=F32)
        ha = ha + gexp * o_ref[...]
    mw = hm_ref.shape[1]
    y = (jnp.dot(hm_ref[...].astype(BF16), wo_ref[0:mw, :], preferred_element_type=F32)
         + jnp.dot(ha.astype(BF16), wo_ref[mw:, :], preferred_element_type=F32))
    x1 = x_ref[...] + g1_ref[0] * y
    x1_ref[...] = x1
    r = x1 * lax.rsqrt(jnp.mean(x1 * x1, axis=-1, keepdims=True) + EPS)
    h2_ref[...] = (r * n2_ref[...]) * (1.0 + sc2_ref[0]) + sh2_ref[0]


def _gate_expand_table():
    ex = np.zeros((3, LANES, A_WIDTH), np.float32)
    for hd in range(A_HEADS):
        for br in range(3):
            ex[br, hd * 3 + br, hd * A_HEAD_DIM:(hd + 1) * A_HEAD_DIM] = 1.0
    return jnp.asarray(ex).astype(BF16)


def _out_projection(hm2, oc2, os2, ow2, z2, x2, w_out, gate1, norm2_g, scale2, shift2, seq):
    n, d = x2.shape
    tm = min(256, seq)
    row = lambda w: pl.BlockSpec((tm, w), lambda i: (i, 0))
    per_b = pl.BlockSpec((1, 1, d), lambda i: ((i * tm) // seq, 0, 0))
    return pl.pallas_call(
        _outproj_kernel,
        grid=(n // tm,),
        in_specs=[row(M_WIDTH), row(A_WIDTH), row(A_WIDTH), row(A_WIDTH),
                  pl.BlockSpec((tm, LANES), lambda i: (i, Z_AG // LANES)),
                  pl.BlockSpec((3, LANES, A_WIDTH), lambda i: (0, 0, 0)),
                  row(d),
                  pl.BlockSpec((M_WIDTH + A_WIDTH, d), lambda i: (0, 0)),
                  per_b,
                  pl.BlockSpec((1, d), lambda i: (0, 0)),
                  per_b, per_b],
        out_specs=[row(d), row(d)],
        out_shape=[jax.ShapeDtypeStruct((n, d), F32)] * 2,
        compiler_params=_cparams("parallel"),
        name="out_projection",
    )(hm2, oc2, os2, ow2, z2, _gate_expand_table(), x2, w_out.astype(BF16), gate1, norm2_g.reshape(1, d),
      scale2, shift2)


def _peer_route_kernel(h2_ref, wqt_ref, sk_ref, eidx_ref, gate_ref):
    tm = h2_ref.shape[0]
    nt = (((1,), (1,)), ((), ()))
    ninf = -3.4e38
    qt = lax.dot_general(wqt_ref[...], h2_ref[...].astype(BF16), nt, preferred_element_type=F32)
    k = P_TOPK
    rowk = lax.broadcasted_iota(jnp.int32, (k, tm), 0)

    def top_rows(sc, payload):
        nrow = sc.shape[0]
        rows = lax.broadcasted_iota(jnp.int32, (nrow, tm), 0)

        def body(r, carry):
            sc, vals, pay = carry
            mx = jnp.max(sc, axis=0, keepdims=True)
            idx = jnp.min(jnp.where(sc == mx, rows, nrow), axis=0, keepdims=True)
            pick = rows == idx
            got = idx if payload is None else jnp.sum(jnp.where(pick, payload, 0), axis=0, keepdims=True)
            vals = jnp.where(rowk == r, mx, vals)
            pay = jnp.where(rowk == r, got, pay)
            return jnp.where(pick, ninf, sc), vals, pay

        _, vals, pay = lax.fori_loop(0, k, body, (sc, jnp.zeros((k, tm), F32), jnp.zeros((k, tm), jnp.int32)))
        return vals, pay

    s1, i1 = top_rows(jnp.dot(sk_ref[0], qt[0:P_HALF], preferred_element_type=F32, precision=HIGHEST), None)
    s2, i2 = top_rows(jnp.dot(sk_ref[1], qt[P_HALF:], preferred_element_type=F32, precision=HIGHEST), None)
    row8 = lax.broadcasted_iota(jnp.int32, (8, tm), 0)
    cand = [s1[0:1] + s2]
    cidx = [i1[0:1] * P_NKEYS + i2]
    for a in range(1, 8):
        keep = row8 < (k // (a + 1))
        cand.append(jnp.where(keep, s1[a:a + 1] + s2[0:8], ninf))
        cidx.append(jnp.where(keep, i1[a:a + 1] * P_NKEYS + i2[0:8], 0))
    cand.append(s1[8:k] + s2[0:1])
    cidx.append(i1[8:k] * P_NKEYS + i2[0:1])
    top, e = top_rows(jnp.concatenate(cand, axis=0), jnp.concatenate(cidx, axis=0))
    ex = jnp.exp(top - top[0:1])
    eidx_ref[0] = e
    gate_ref[0] = ex / jnp.sum(ex, axis=0, keepdims=True)


def _peer_route(h2, wq, subkeys, seq):
    n, d = h2.shape
    tm = min(256, seq)
    wqt = wq.T.astype(BF16)
    out = pl.BlockSpec((1, P_TOPK, tm), lambda i, h: (h, 0, i))
    return pl.pallas_call(
        _peer_route_kernel,
        grid=(n // tm, P_HEADS),
        in_specs=[pl.BlockSpec((tm, d), lambda i, h: (i, 0)),
                  pl.BlockSpec((P_QDIM, d), lambda i, h: (h, 0)),
                  pl.BlockSpec((2, P_NKEYS, P_HALF), lambda i, h: (0, 0, 0))],
        out_specs=[out, out],
        out_shape=[jax.ShapeDtypeStruct((P_HEADS, P_TOPK, n), jnp.int32),
                   jax.ShapeDtypeStruct((P_HEADS, P_TOPK, n), F32)],
        compiler_params=_cparams("parallel", "arbitrary"),
        name="peer_route",
    )(h2, wqt, subkeys)


PEER_TB = 8
PEER_SEL = P_HEADS * P_TOPK


PEER_HALF = PEER_TB // 2


def _peer_expert_kernel(eidx_ref, enext_ref, gt_ref, h2_ref, x1_ref, g2_ref, tab_ref, o_ref, buf, sem):
    i = pl.program_id(0)
    d = h2_ref.shape[1]
    rows = PEER_HALF * PEER_SEL

    def issue(idx_ref, tok, dst_slot, t):
        for j in range(PEER_SEL):
            e = idx_ref[tok, j]
            pltpu.make_async_copy(tab_ref.at[pl.ds(e, 1), :],
                                  buf.at[dst_slot, pl.ds(t * PEER_SEL + j, 1), :],
                                  sem.at[dst_slot]).start()

    def wait(s):
        pltpu.make_async_copy(tab_ref.at[pl.ds(0, rows), :], buf.at[s], sem.at[s]).wait()

    def mix(s, t, tok):
        w = buf[s, t * PEER_SEL:(t + 1) * PEER_SEL, :]
        a = jnp.sum(w[:, :d] * h2_ref[tok:tok + 1, :], axis=1, keepdims=True)
        wg = gt_ref[0][:, tok:tok + 1] * _gelu_tanh(a)
        y = jnp.sum(w[:, d:] * wg, axis=0, keepdims=True)
        o_ref[tok:tok + 1, :] = x1_ref[tok:tok + 1, :] + g2_ref[0] * y

    @pl.when(i == 0)
    def _():
        for t in range(PEER_HALF):
            issue(eidx_ref, t, 0, t)

    wait(0)
    for t in range(PEER_HALF):
        issue(eidx_ref, PEER_HALF + t, 1, t)
        mix(0, t, t)
    wait(1)
    for t in range(PEER_HALF):
        issue(enext_ref, t, 0, t)
        mix(1, t, PEER_HALF + t)

    @pl.when(i == pl.num_programs(0) - 1)
    def _():
        wait(0)


def _peer_experts(eidx_t, gate_t, h2, x1, gate2, table, seq):
    n, d = h2.shape
    tb = PEER_TB
    steps = n // tb
    eidx = eidx_t.reshape(PEER_SEL, n).T
    gate_t = gate_t.reshape(PEER_SEL, steps, tb).transpose(1, 0, 2)
    row = pl.BlockSpec((tb, d), lambda i: (i, 0))
    return pl.pallas_call(
        _peer_expert_kernel,
        grid=(steps,),
        in_specs=[pl.BlockSpec((tb, PEER_SEL), lambda i: (i, 0), memory_space=pltpu.SMEM),
                  pl.BlockSpec((tb, PEER_SEL), lambda i: (jnp.minimum(i + 1, steps - 1), 0),
                               memory_space=pltpu.SMEM),
                  pl.BlockSpec((1, PEER_SEL, tb), lambda i: (i, 0, 0)),
                  row, row,
                  pl.BlockSpec((1, 1, d), lambda i: ((i * tb) // seq, 0, 0)),
                  pl.BlockSpec(memory_space=pl.ANY)],
        out_specs=row,
        out_shape=jax.ShapeDtypeStruct((n, d), F32),
        scratch_shapes=[pltpu.VMEM((2, PEER_HALF * PEER_SEL, 2 * d), F32), pltpu.SemaphoreType.DMA((2,))],
        compiler_params=_cparams("arbitrary"),
        name="peer_experts",
    )(eidx, eidx, gate_t, h2, x1, gate2, table)


def kernel(x, c, w_mod, b_mod, norm1_g, norm2_g, w_in, conv_qk, b_igate, b_fgate, mlstm_norm_g, qn_g, kn_g,
           cmp_pos_k, cmp_pos_v, cmp_k_w1, cmp_k_w2, cmp_v_w1, cmp_v_w2, w_out, peer_wq, peer_subkeys,
           peer_u, peer_v):
    b_, s_, d = x.shape
    n = b_ * s_
    for l in range(w_mod.shape[0]):
        mod = _modulation(c, w_mod[l], b_mod[l]).reshape(b_, 6, 1, d)
        shift1, scale1, gate1, shift2, scale2, gate2 = (mod[:, i] for i in range(6))
        x2 = x.reshape(n, d)
        z2 = _in_projection(x2, norm1_g[l], scale1, shift1, _pad_in_weights(w_in[l]), s_)
        z3 = z2.reshape(b_, s_, Z_COLS)
        gate_bias = jnp.concatenate([b_igate[l], b_fgate[l], jnp.zeros((LANES - 2 * M_HEADS,), F32)])
        hm = _mlstm(z3, conv_qk[l], gate_bias.reshape(1, LANES), mlstm_norm_g[l])
        qn, qa, kc, vc, ks, vs, kw, vw = _nsa_prep(z3, qn_g[l], kn_g[l])
        k_cmp = _compress(kc, cmp_pos_k[l], cmp_k_w1[l], cmp_k_w2[l], kn_g[l, 0], True)
        v_cmp = _compress(vc, cmp_pos_v[l], cmp_v_w1[l], cmp_v_w2[l], kn_g[l, 0], False)
        o_cmp, sel = _nsa_cmp(qn, k_cmp, v_cmp)
        o_win, o_sel = _nsa_attn(qn, qa, kw, vw, ks, vs, sel)
        x1, h2 = _out_projection(hm.reshape(n, M_WIDTH), o_cmp.reshape(n, A_WIDTH), o_sel.reshape(n, A_WIDTH),
                                 o_win.reshape(n, A_WIDTH), z2, x2, w_out[l], gate1, norm2_g[l], scale2, shift2, s_)
        eidx, gate = _peer_route(h2, peer_wq[l], peer_subkeys[l], s_)
        table = jnp.concatenate([peer_u[l], peer_v[l]], axis=1)
        x = _peer_experts(eidx, gate, h2, x1, gate2, table, s_).reshape(b_, s_, d)
    return x
```

```python
import functools

import numpy as np
import jax
import jax.numpy as jnp
from jax import lax
from jax.experimental import pallas as pl
from jax.experimental.pallas import tpu as pltpu

F32 = jnp.float32
BF16 = jnp.bfloat16
HIGHEST = lax.Precision.HIGHEST

EPS = 1e-6
D_MODEL = 2048
M_HEADS = 4
M_HEAD_DIM = 256
M_WIDTH = M_HEADS * M_HEAD_DIM
M_CHUNK = 256
CONV_W = 4
A_GROUPS = 4
A_HPG = 4
A_HEADS = A_GROUPS * A_HPG
A_HEAD_DIM = 64
A_WIDTH = A_HEADS * A_HEAD_DIM
CMP_BLOCK = 32
CMP_STRIDE = 16
CMP_HIDDEN = 256
SEL_BLOCK = 64
SEL_TOP = 16
WINDOW = 512
Q_BLOCK = 128
FORCE_SCORE = 1e4
P_HEADS = 8
P_NKEYS = 128
P_TOPK = 16
P_QDIM = 256
P_HALF = 128
MV_OFF = 2 * M_WIDTH
MO_OFF = 3 * M_WIDTH
MIF_OFF = 4 * M_WIDTH
AQ_OFF = MIF_OFF + 2 * M_HEADS
AKV_OFF = AQ_OFF + A_WIDTH
AG_OFF = AKV_OFF + 6 * A_GROUPS * A_HEAD_DIM
IN_COLS = AG_OFF + 3 * A_HEADS
Z_M = 0
Z_AQ = 4096
Z_KV = 5120
Z_IF = 6656
Z_AG = 6784
Z_COLS = 6912
LANES = 128
NEG = -1e30
VMEM_LIMIT = 56 * 1024 * 1024


def _cparams(*sem):
    return pltpu.CompilerParams(dimension_semantics=sem, vmem_limit_bytes=VMEM_LIMIT)


def _gelu_tanh(x):
    return 0.5 * x * (1.0 + jnp.tanh(np.sqrt(2.0 / np.pi).astype(np.float32) * (x + 0.044715 * (x * x * x))))


def _mod_kernel(c_ref, w_ref, b_ref, o_ref):
    c = c_ref[...]
    s = c * jax.nn.sigmoid(c)
    o_ref[...] = jnp.dot(s, w_ref[...], preferred_element_type=F32, precision=HIGHEST) + b_ref[...]


def _modulation(c, w_mod, b_mod):
    b_, d = c.shape
    n = w_mod.shape[1]
    tn = 1024
    cp = jnp.pad(c, ((0, 8 - b_), (0, 0)))
    out = pl.pallas_call(
        _mod_kernel,
        grid=(n // tn,),
        in_specs=[pl.BlockSpec((8, d), lambda j: (0, 0)),
                  pl.BlockSpec((d, tn), lambda j: (0, j)),
                  pl.BlockSpec((1, tn), lambda j: (0, j))],
        out_specs=pl.BlockSpec((8, tn), lambda j: (0, j)),
        out_shape=jax.ShapeDtypeStruct((8, n), F32),
        compiler_params=_cparams("parallel"),
        name="modulation",
    )(cp, w_mod, b_mod.reshape(1, n))
    return out[:b_]


def _inproj_kernel(x_ref, g_ref, sc_ref, sh_ref, w_ref, o_ref, h_scr):
    @pl.when(pl.program_id(1) == 0)
    def _():
        x = x_ref[...]
        r = x * lax.rsqrt(jnp.mean(x * x, axis=-1, keepdims=True) + EPS)
        h = (r * g_ref[...]) * (1.0 + sc_ref[0]) + sh_ref[0]
        h_scr[...] = h.astype(BF16)

    o_ref[...] = jnp.dot(h_scr[...], w_ref[...], preferred_element_type=F32)


def _in_projection(x2, g1, scale1, shift1, w_pad, seq):
    n, d = x2.shape
    zc = w_pad.shape[1]
    tm = min(512, seq)
    tn = 768
    return pl.pallas_call(
        _inproj_kernel,
        grid=(n // tm, zc // tn),
        in_specs=[pl.BlockSpec((tm, d), lambda i, j: (i, 0)),
                  pl.BlockSpec((1, d), lambda i, j: (0, 0)),
                  pl.BlockSpec((1, 1, d), lambda i, j: ((i * tm) // seq, 0, 0)),
                  pl.BlockSpec((1, 1, d), lambda i, j: ((i * tm) // seq, 0, 0)),
                  pl.BlockSpec((d, tn), lambda i, j: (0, j))],
        out_specs=pl.BlockSpec((tm, tn), lambda i, j: (i, j)),
        out_shape=jax.ShapeDtypeStruct((n, zc), F32),
        scratch_shapes=[pltpu.VMEM((tm, d), BF16)],
        compiler_params=_cparams("parallel", "arbitrary"),
        name="in_projection",
    )(x2, g1.reshape(1, d), scale1, shift1, w_pad)


def _pad_in_weights(w_in):
    d = w_in.shape[0]
    z = lambda k: jnp.zeros((d, k), w_in.dtype)
    w = jnp.concatenate([
        w_in[:, :MIF_OFF],
        w_in[:, AQ_OFF:AKV_OFF],
        w_in[:, AKV_OFF:AG_OFF],
        w_in[:, MIF_OFF:AQ_OFF], z(LANES - 2 * M_HEADS),
        w_in[:, AG_OFF:IN_COLS], z(LANES - 3 * A_HEADS),
    ], axis=1)
    return w.astype(BF16)


def _mlstm_kernel(q_ref, k_ref, v_ref, o_ref, if_ref, cq_ref, ck_ref, bias_ref, g_ref, out_ref,
                  c_scr, n_scr, m_scr, qt_scr, kt_scr):
    L, d = q_ref.shape[1], q_ref.shape[2]
    h = pl.program_id(1)

    @pl.when(pl.program_id(2) == 0)
    def _():
        c_scr[...] = jnp.zeros_like(c_scr)
        n_scr[...] = jnp.zeros_like(n_scr)
        m_scr[...] = jnp.zeros_like(m_scr)
        qt_scr[...] = jnp.zeros_like(qt_scr)
        kt_scr[...] = jnp.zeros_like(kt_scr)

    def conv_silu(raw, tail_scr, w_ref):
        ext = jnp.concatenate([tail_scr[...], raw], axis=0)
        acc = jnp.zeros((L, d), F32)
        for j in range(CONV_W):
            off = 8 - (CONV_W - 1) + j
            acc = acc + w_ref[j:j + 1, :] * ext[off:off + L, :]
        tail_scr[...] = raw[L - 8:, :]
        return acc * jax.nn.sigmoid(acc)

    qc = conv_silu(q_ref[0], qt_scr, cq_ref)
    kc = conv_silu(k_ref[0], kt_scr, ck_ref) * (d ** -0.5)
    vc = v_ref[0]

    gt = if_ref[0] + bias_ref[...]
    lane = lax.broadcasted_iota(jnp.int32, (L, LANES), 1)
    logsig = jnp.minimum(gt, 0.0) - jnp.log(1.0 + jnp.exp(-jnp.abs(gt)))
    gt = jnp.where(lane < M_HEADS, gt, logsig)
    rows = lax.broadcasted_iota(jnp.int32, (L, L), 0)
    cols = lax.broadcasted_iota(jnp.int32, (L, L), 1)
    causal = rows >= cols
    tril = jnp.where(causal, 1.0, 0.0).astype(F32)
    csum = jnp.dot(tril, gt, preferred_element_type=F32, precision=HIGHEST)
    gt_t = gt.T
    csum_t = csum.T
    sub = lax.broadcasted_iota(jnp.int32, (LANES, L), 0)
    ig_col = jnp.sum(jnp.where(lane == h, gt, 0.0), axis=1, keepdims=True)
    b_col = jnp.sum(jnp.where(lane == h + M_HEADS, csum, 0.0), axis=1, keepdims=True)
    ig_row = jnp.sum(jnp.where(sub == h, gt_t, 0.0), axis=0, keepdims=True)
    b_row = jnp.sum(jnp.where(sub == h + M_HEADS, csum_t, 0.0), axis=0, keepdims=True)

    m_st = m_scr[0:1, 0:1]
    dmat = jnp.where(causal, b_col - b_row + ig_row, NEG)
    m_inter = b_col + m_st
    m_t = jnp.maximum(m_inter, jnp.max(dmat, axis=1, keepdims=True))
    qb = qc.astype(BF16)
    kb = kc.astype(BF16)
    vb = vc.astype(BF16)
    qk = lax.dot_general(qb, kb, (((1,), (1,)), ((), ())), preferred_element_type=F32)
    w = jnp.exp(dmat - m_t) * qk
    a_inter = jnp.exp(m_inter - m_t)
    c_st = c_scr[...]
    n_st = n_scr[...]
    num = (jnp.dot(w.astype(BF16), vb, preferred_element_type=F32)
           + a_inter * jnp.dot(qb, c_st.astype(BF16), preferred_element_type=F32))
    den = jnp.sum(w, axis=1, keepdims=True) + a_inter * jnp.sum(qc * n_st, axis=1, keepdims=True)
    hout = num / jnp.maximum(jnp.abs(den), jnp.exp(-m_t))

    b_last = b_col[L - 1:L, :]
    a_s = b_last - b_col + ig_col
    m_new = jnp.maximum(b_last + m_st, jnp.max(a_s, axis=0, keepdims=True))
    w_s = jnp.exp(a_s - m_new)
    decay = jnp.exp(b_last + m_st - m_new)
    kw = kc * w_s
    c_scr[...] = decay * c_st + lax.dot_general(kw.astype(BF16), vb, (((0,), (0,)), ((), ())),
                                                preferred_element_type=F32)
    n_scr[...] = decay * n_st + jnp.sum(kw, axis=0, keepdims=True)
    m_scr[...] = jnp.broadcast_to(m_new, m_scr.shape)

    r = hout * lax.rsqrt(jnp.mean(hout * hout, axis=-1, keepdims=True) + EPS)
    out_ref[0] = (r * g_ref[...]) * jax.nn.sigmoid(o_ref[0])


def _mlstm(z3, conv_qk, gate_bias, norm_g):
    b_, s_, _ = z3.shape
    L = min(M_CHUNK, s_)
    d = M_HEAD_DIM
    H = M_HEADS
    blk = lambda off: pl.BlockSpec((1, L, d), lambda b, h, c: (b, c, off + h))
    return pl.pallas_call(
        _mlstm_kernel,
        grid=(b_, H, s_ // L),
        in_specs=[blk(0), blk(H), blk(2 * H), blk(3 * H),
                  pl.BlockSpec((1, L, LANES), lambda b, h, c: (b, c, Z_IF // LANES)),
                  pl.BlockSpec((CONV_W, d), lambda b, h, c: (0, h)),
                  pl.BlockSpec((CONV_W, d), lambda b, h, c: (0, H + h)),
                  pl.BlockSpec((1, LANES), lambda b, h, c: (0, 0)),
                  pl.BlockSpec((1, d), lambda b, h, c: (0, h))],
        out_specs=pl.BlockSpec((1, L, d), lambda b, h, c: (b, c, h)),
        out_shape=jax.ShapeDtypeStruct((b_, s_, M_WIDTH), F32),
        scratch_shapes=[pltpu.VMEM((d, d), F32), pltpu.VMEM((1, d), F32), pltpu.VMEM((8, LANES), F32),
                        pltpu.VMEM((8, d), F32), pltpu.VMEM((8, d), F32)],
        compiler_params=_cparams("parallel", "parallel", "arbitrary"),
        name="mlstm",
    )(z3, z3, z3, z3, z3, conv_qk, conv_qk, gate_bias, norm_g.reshape(1, M_WIDTH))


SEL_TILE = 512
LOG2E = float(np.log2(np.e))


def _nsa_prep_kernel(aq_ref, kc_ref, vc_ref, ks_ref, vs_ref, kw_ref, vw_ref, qg_ref, kg_ref, slc_ref,
                     qn_o, qa_o, kc_o, vc_o, ks_o, vs_o, kw_o, vw_o):
    hd = A_HEAD_DIM
    ts = aq_ref.shape[1]

    def norm(xh, g):
        return (xh * lax.rsqrt(jnp.mean(xh * xh, axis=-1, keepdims=True) + EPS)) * g

    aq = aq_ref[0]
    qg = qg_ref[...]
    for h in range(A_HEADS):
        qh = norm(aq[:, h * hd:(h + 1) * hd], qg) * (hd ** -0.5)
        qn_o[0, h // A_HPG, h % A_HPG] = qh.astype(BF16)
        slope_cols = jnp.broadcast_to(slc_ref[h:h + 1, :], (ts, hd))
        qa_o[0, h // A_HPG, h % A_HPG] = jnp.concatenate([qh * LOG2E, slope_cols], axis=1).astype(BF16)
    pos = pl.program_id(1) * ts + lax.broadcasted_iota(jnp.int32, (ts, hd), 0)
    lane = lax.broadcasted_iota(jnp.int32, (ts, hd), 1)
    c = pos & (SEL_TILE - 1)
    kind = lambda r: (lane == r) | (lane == r + 3) | (lane == r + 6)
    pos_cols = jnp.where(kind(0), c & 255, jnp.where(kind(1), c & 256, jnp.where(kind(2), pos - c, 0))).astype(F32)
    kc, vc, ks, vs, kw, vw = kc_ref[0], vc_ref[0], ks_ref[0], vs_ref[0], kw_ref[0], vw_ref[0]
    for g in range(A_GROUPS):
        sl = slice(g * hd, (g + 1) * hd)
        kc_o[0, g] = kc[:, sl]
        vc_o[0, g] = vc[:, sl]
        ks_o[0, g] = jnp.concatenate([norm(ks[:, sl], kg_ref[1:2, :]), pos_cols], axis=1).astype(BF16)
        vs_o[0, g] = vs[:, sl].astype(BF16)
        kw_o[0, g] = norm(kw[:, sl], kg_ref[2:3, :]).astype(BF16)
        vw_o[0, g] = vw[:, sl].astype(BF16)


def _slope_cols_table():
    slopes = 2.0 ** (-8.0 * (np.arange(A_HEADS) + 1) / A_HEADS) * np.log2(np.e)
    tab = np.zeros((A_HEADS, A_HEAD_DIM), np.float32)
    rem = slopes.astype(np.float64)
    for i in range(3):
        part = rem.astype(np.float32).astype(BF16).astype(np.float64)
        tab[:, 3 * i:3 * i + 3] = part[:, None]
        rem = rem - part
    return jnp.asarray(tab)


def _nsa_prep(z3, qn_g, kn_g):
    b_, s_, _ = z3.shape
    ts = min(512, s_)
    G, hd = A_GROUPS, A_HEAD_DIM
    kvw = G * hd
    kv_in = lambda i: pl.BlockSpec((1, ts, kvw), lambda b, t: (b, t, Z_KV // kvw + i))
    kv_out = lambda w: pl.BlockSpec((1, G, ts, w), lambda b, t: (b, 0, t, 0))
    q_out = lambda w: pl.BlockSpec((1, G, A_HPG, ts, w), lambda b, t: (b, 0, 0, t, 0))
    kv_shape = lambda dt, w=hd: jax.ShapeDtypeStruct((b_, G, s_, w), dt)
    return pl.pallas_call(
        _nsa_prep_kernel,
        grid=(b_, s_ // ts),
        in_specs=[pl.BlockSpec((1, ts, A_WIDTH), lambda b, t: (b, t, Z_AQ // A_WIDTH))]
                 + [kv_in(i) for i in range(6)]
                 + [pl.BlockSpec((1, hd), lambda b, t: (0, 0)), pl.BlockSpec((3, hd), lambda b, t: (0, 0)),
                    pl.BlockSpec((A_HEADS, hd), lambda b, t: (0, 0))],
        out_specs=[q_out(hd), q_out(2 * hd), kv_out(hd), kv_out(hd), kv_out(2 * hd), kv_out(hd), kv_out(hd),
                   kv_out(hd)],
        out_shape=[jax.ShapeDtypeStruct((b_, G, A_HPG, s_, hd), BF16),
                   jax.ShapeDtypeStruct((b_, G, A_HPG, s_, 2 * hd), BF16),
                   kv_shape(F32), kv_shape(F32), kv_shape(BF16, 2 * hd), kv_shape(BF16), kv_shape(BF16),
                   kv_shape(BF16)],
        compiler_params=_cparams("parallel", "parallel"),
        name="nsa_prep",
    )(z3, z3, z3, z3, z3, z3, z3, qn_g.reshape(1, hd), kn_g, _slope_cols_table())


def _compress_kernel(r_ref, pos_ref, w1a_ref, w1b_ref, w2_ref, g_ref, o_ref, *, do_norm):
    r = r_ref[0, 0]
    nr = r.shape[0]
    u = jnp.dot((r + pos_ref[0:1, :]).astype(BF16), w1a_ref[...], preferred_element_type=F32)
    v = jnp.dot((r + pos_ref[1:2, :]).astype(BF16), w1b_ref[...], preferred_element_type=F32)
    pre = u + pltpu.roll(v, nr - 1, 0)
    out = jnp.dot(_gelu_tanh(pre).astype(BF16), w2_ref[...], preferred_element_type=F32)
    if do_norm:
        out = (out * lax.rsqrt(jnp.mean(out * out, axis=-1, keepdims=True) + EPS)) * g_ref[...]
    o_ref[0, 0] = out.astype(BF16)


def _compress(a, pos, w1, w2, g, do_norm):
    b_, G, s_, hd = a.shape
    nr = s_ // CMP_STRIDE
    half = CMP_STRIDE * hd
    r = a.reshape(b_, G, nr, half)
    w1b16 = w1.astype(BF16)
    return pl.pallas_call(
        functools.partial(_compress_kernel, do_norm=do_norm),
        grid=(b_, G),
        in_specs=[pl.BlockSpec((1, 1, nr, half), lambda b, g: (b, g, 0, 0)),
                  pl.BlockSpec((2, half), lambda b, g: (0, 0)),
                  pl.BlockSpec((half, CMP_HIDDEN), lambda b, g: (0, 0)),
                  pl.BlockSpec((half, CMP_HIDDEN), lambda b, g: (1, 0)),
                  pl.BlockSpec((CMP_HIDDEN, hd), lambda b, g: (0, 0)),
                  pl.BlockSpec((1, hd), lambda b, g: (0, 0))],
        out_specs=pl.BlockSpec((1, 1, nr, hd), lambda b, g: (b, g, 0, 0)),
        out_shape=jax.ShapeDtypeStruct((b_, G, nr, hd), BF16),
        compiler_params=_cparams("parallel", "parallel"),
        name="nsa_compress_norm" if do_norm else "nsa_compress",
    )(r, pos.reshape(2, half), w1b16, w1b16, w2.astype(BF16), g.reshape(1, hd))


def _nsa_cmp_kernel(q_ref, kc_ref, vc_ref, ov_ref, slope_ref, ocmp_ref, sel_ref, *, n_cmp, n_blk, n_sel):
    T, hd = q_ref.shape[3], q_ref.shape[4]
    R = A_HPG * T
    nc = kc_ref.shape[2]
    q0 = pl.program_id(2) * T
    q = q_ref[0, 0].reshape(R, hd)
    s = lax.dot_general(q, kc_ref[0, 0], (((1,), (1,)), ((), ())), preferred_element_type=F32)
    row = lax.broadcasted_iota(jnp.int32, (R, nc), 0)
    ci = lax.broadcasted_iota(jnp.int32, (R, nc), 1)
    t = q0 + (row & (T - 1))
    disti = t - (ci * CMP_STRIDE + (CMP_BLOCK - 1))
    valid = (disti >= 0) & (ci < n_cmp)
    slope = slope_ref[0][:, 0:1]
    s = jnp.where(valid, s - slope * disti.astype(F32), NEG)
    m = jnp.max(s, axis=1, keepdims=True)
    e = jnp.where(valid, jnp.exp(s - m), 0.0)
    p = e / jnp.maximum(jnp.sum(e, axis=1, keepdims=True), 1e-30)
    oc = jnp.dot(p.astype(BF16), vc_ref[0, 0], preferred_element_type=F32)
    for h in range(A_HPG):
        ocmp_ref[0, :, h * hd:(h + 1) * hd] = oc[h * T:(h + 1) * T, :]
    ps = p[0:T] + p[T:2 * T] + p[2 * T:3 * T] + p[3 * T:4 * T]
    hi = ps.astype(BF16)
    lo = (ps - hi.astype(F32)).astype(BF16)
    ov = ov_ref[...]
    imp = jnp.dot(hi, ov, preferred_element_type=F32) + jnp.dot(lo, ov, preferred_element_type=F32)
    imp = imp.T
    blk = lax.broadcasted_iota(jnp.int32, (LANES, T), 0)
    cur = (q0 + lax.broadcasted_iota(jnp.int32, (LANES, T), 1)) // SEL_BLOCK
    forced = (blk == 0) | (blk == cur) | (blk == cur - 1)
    imp = jnp.where(forced, FORCE_SCORE, jnp.where(blk <= cur, imp, -FORCE_SCORE))
    imp = jnp.where(blk < n_blk, imp, -3e38)
    sel = jnp.zeros((LANES, T), F32)
    for _ in range(n_sel):
        mx = jnp.max(imp, axis=0, keepdims=True)
        idx = jnp.min(jnp.where(imp == mx, blk, LANES), axis=0, keepdims=True)
        pick = blk == idx
        sel = jnp.where(pick, 1.0, sel)
        imp = jnp.where(pick, -3.4e38, imp)
    sel_ref[0, 0] = sel.T.astype(BF16)


def _slope_table():
    slopes = (2.0 ** (-8.0 * (np.arange(A_HEADS) + 1) / A_HEADS)).astype(np.float32).reshape(A_GROUPS, A_HPG)
    tab = np.repeat(slopes, Q_BLOCK, axis=1)
    return jnp.asarray(np.broadcast_to(tab[:, :, None], (A_GROUPS, A_HPG * Q_BLOCK, LANES)).copy())


def _overlap_table(s_, nc):
    n_cmp = (s_ - CMP_BLOCK) // CMP_STRIDE + 1
    n_blk = s_ // SEL_BLOCK
    starts = np.arange(nc)[:, None] * CMP_STRIDE
    blk_starts = np.arange(LANES)[None, :] * SEL_BLOCK
    ov = (starts < blk_starts + SEL_BLOCK) & (starts + CMP_BLOCK > blk_starts)
    ov &= (np.arange(nc)[:, None] < n_cmp) & (np.arange(LANES)[None, :] < n_blk)
    return jnp.asarray(ov.astype(np.float32)).astype(BF16)


def _nsa_cmp(qn, k_cmp, v_cmp):
    b_, G, hpg, s_, hd = qn.shape
    T = Q_BLOCK
    nc = k_cmp.shape[2]
    n_cmp = (s_ - CMP_BLOCK) // CMP_STRIDE + 1
    n_blk = s_ // SEL_BLOCK
    kern = functools.partial(_nsa_cmp_kernel, n_cmp=n_cmp, n_blk=n_blk, n_sel=min(SEL_TOP, n_blk))
    return pl.pallas_call(
        kern,
        grid=(b_, G, s_ // T),
        in_specs=[pl.BlockSpec((1, 1, hpg, T, hd), lambda b, g, i: (b, g, 0, i, 0)),
                  pl.BlockSpec((1, 1, nc, hd), lambda b, g, i: (b, g, 0, 0)),
                  pl.BlockSpec((1, 1, nc, hd), lambda b, g, i: (b, g, 0, 0)),
                  pl.BlockSpec((nc, LANES), lambda b, g, i: (0, 0)),
                  pl.BlockSpec((1, hpg * T, LANES), lambda b, g, i: (g, 0, 0))],
        out_specs=[pl.BlockSpec((1, T, hpg * hd), lambda b, g, i: (b, i, g)),
                   pl.BlockSpec((1, 1, T, LANES), lambda b, g, i: (b, g, i, 0))],
        out_shape=[jax.ShapeDtypeStruct((b_, s_, A_WIDTH), F32),
                   jax.ShapeDtypeStruct((b_, G, s_, LANES), BF16)],
        compiler_params=_cparams("parallel", "parallel", "parallel"),
        name="nsa_cmp_topk",
    )(qn, k_cmp, v_cmp, _overlap_table(s_, nc), _slope_table())


def _nsa_attn_kernel(q_ref, qa_ref, kw_ref, vw_ref, ks_ref, vs_ref, sel_ref, slope_ref, owin_ref, osel_ref,
                     m_scr, l_scr, acc_scr, *, wk, tk):
    T, hd = q_ref.shape[3], q_ref.shape[4]
    R = A_HPG * T
    qb = pl.program_id(2)
    q0 = qb * T
    q = q_ref[0, 0].reshape(R, hd)
    slope = slope_ref[0][:, 0:1]
    nt = (((1,), (1,)), ((), ()))

    start = pl.multiple_of(jnp.maximum(q0 + T - wk, 0), T)
    kw = kw_ref[0, 0, pl.ds(start, wk), :]
    vw = vw_ref[0, 0, pl.ds(start, wk), :]
    s = lax.dot_general(q, kw, nt, preferred_element_type=F32)
    t = q0 + (lax.broadcasted_iota(jnp.int32, (R, wk), 0) & (T - 1))
    disti = t - (start + lax.broadcasted_iota(jnp.int32, (R, wk), 1))
    mask = (disti >= 0) & (disti < WINDOW)
    s = jnp.where(mask, s - slope * disti.astype(F32), NEG)
    m = jnp.max(s, axis=1, keepdims=True)
    e = jnp.exp(s - m)
    ow = jnp.dot(e.astype(BF16), vw, preferred_element_type=F32) / jnp.sum(e, axis=1, keepdims=True)
    for h in range(A_HPG):
        owin_ref[0, :, h * hd:(h + 1) * hd] = ow[h * T:(h + 1) * T, :]

    qa = qa_ref[0, 0].reshape(R, 2 * hd)
    sel = sel_ref[0, 0]
    sel_bias = ((sel.astype(F32) - 1.0) * 1e30).astype(BF16)
    m_scr[...] = jnp.full_like(m_scr, NEG)
    l_scr[...] = jnp.zeros_like(l_scr)
    acc_scr[...] = jnp.zeros_like(acc_scr)
    bpt = tk // SEL_BLOCK
    lane = lax.broadcasted_iota(jnp.int32, (T, LANES), 1)
    j_diag = q0 // tk

    def tile(j, diagonal):
        kv0 = pl.multiple_of(j * tk, tk)
        bi = lax.broadcasted_iota(jnp.int32, (LANES, tk), 0)
        cc = lax.broadcasted_iota(jnp.int32, (LANES, tk), 1)
        expand = jnp.where(((kv0 + cc) // SEL_BLOCK) == bi, 1.0, 0.0).astype(BF16)
        bias = jnp.dot(sel_bias, expand, preferred_element_type=F32)
        k = ks_ref[0, 0, pl.ds(kv0, tk), :]
        v = vs_ref[0, 0, pl.ds(kv0, tk), :]
        sc = lax.dot_general(qa, k, nt, preferred_element_type=F32) + jnp.concatenate([bias] * A_HPG, axis=0)
        if diagonal:
            tt = q0 + (lax.broadcasted_iota(jnp.int32, (R, tk), 0) & (T - 1))
            sc = jnp.where(tt >= kv0 + lax.broadcasted_iota(jnp.int32, (R, tk), 1), sc, NEG)
        m_old = m_scr[...]
        m_new = jnp.maximum(m_old, jnp.max(sc, axis=1, keepdims=True))
        alpha = jnp.exp2(m_old - m_new)
        p = jnp.exp2(sc - m_new)
        l_scr[...] = alpha * l_scr[...] + jnp.sum(p, axis=1, keepdims=True)
        acc_scr[...] = alpha * acc_scr[...] + jnp.dot(p.astype(BF16), v, preferred_element_type=F32)
        m_scr[...] = m_new

    def body(j, carry):
        in_tile = (lane >= j * bpt) & (lane < (j + 1) * bpt)
        has = jnp.max(jnp.where(in_tile, sel.astype(F32), 0.0)) > 0.5

        @pl.when(jnp.logical_or(has, j == 0))
        def _():
            tile(j, False)

        return carry

    lax.fori_loop(0, j_diag, body, 0)
    tile(j_diag, True)
    os_ = acc_scr[...] / l_scr[...]
    for h in range(A_HPG):
        osel_ref[0, :, h * hd:(h + 1) * hd] = os_[h * T:(h + 1) * T, :]


def _nsa_attn(qn, qa, k_win, v_win, k_sel_aug, v_sel, sel):
    b_, G, hpg, s_, hd = qn.shape
    T = Q_BLOCK
    wk = min(WINDOW + T, s_)
    tk = SEL_TILE
    assert s_ % tk == 0
    full = lambda w: pl.BlockSpec((1, 1, s_, w), lambda b, g, i: (b, g, 0, 0))
    q_blk = lambda w: pl.BlockSpec((1, 1, hpg, T, w), lambda b, g, i: (b, g, 0, i, 0))
    out = pl.BlockSpec((1, T, hpg * hd), lambda b, g, i: (b, i, g))
    return pl.pallas_call(
        functools.partial(_nsa_attn_kernel, wk=wk, tk=tk),
        grid=(b_, G, s_ // T),
        in_specs=[q_blk(hd), q_blk(2 * hd),
                  full(hd), full(hd), full(2 * hd), full(hd),
                  pl.BlockSpec((1, 1, T, LANES), lambda b, g, i: (b, g, i, 0)),
                  pl.BlockSpec((1, hpg * T, LANES), lambda b, g, i: (g, 0, 0))],
        out_specs=[out, out],
        out_shape=[jax.ShapeDtypeStruct((b_, s_, A_WIDTH), F32)] * 2,
        scratch_shapes=[pltpu.VMEM((hpg * T, 1), F32), pltpu.VMEM((hpg * T, 1), F32),
                        pltpu.VMEM((hpg * T, hd), F32)],
        compiler_params=_cparams("parallel", "parallel", "arbitrary"),
        name="nsa_window_selected",
    )(qn, qa, k_win, v_win, k_sel_aug, v_sel, sel, _slope_table())


def _outproj_kernel(hm_ref, oc_ref, os_ref, ow_ref, ag_ref, ex_ref, x_ref, wo_ref, g1_ref, n2_ref, sc2_ref,
                    sh2_ref, x1_ref, h2_ref):
    sg = jax.nn.sigmoid(ag_ref[...])
    hi = sg.astype(BF16)
    lo = (sg - hi.astype(F32)).astype(BF16)
    ha = jnp.zeros(oc_ref.shape, F32)
    for br, o_ref in enumerate((oc_ref, os_ref, ow_ref)):
        ex = ex_ref[br]
        gexp = jnp.dot(hi, ex, preferred_element_type=F32) + jnp.dot(lo, ex, preferred_element_type=F32)
        ha = ha + gexp * o_ref[...]
    mw = hm_ref.shape[1]
    y = (jnp.dot(hm_ref[...].astype(BF16), wo_ref[0:mw, :], preferred_element_type=F32)
         + jnp.dot(ha.astype(BF16), wo_ref[mw:, :], preferred_element_type=F32))
    x1 = x_ref[...] + g1_ref[0] * y
    x1_ref[...] = x1
    r = x1 * lax.rsqrt(jnp.mean(x1 * x1, axis=-1, keepdims=True) + EPS)
    h2_ref[...] = (r * n2_ref[...]) * (1.0 + sc2_ref[0]) + sh2_ref[0]


def _gate_expand_table():
    ex = np.zeros((3, LANES, A_WIDTH), np.float32)
    for hd in range(A_HEADS):
        for br in range(3):
            ex[br, hd * 3 + br, hd * A_HEAD_DIM:(hd + 1) * A_HEAD_DIM] = 1.0
    return jnp.asarray(ex).astype(BF16)


def _out_projection(hm2, oc2, os2, ow2, z2, x2, w_out, gate1, norm2_g, scale2, shift2, seq):
    n, d = x2.shape
    tm = min(256, seq)
    row = lambda w: pl.BlockSpec((tm, w), lambda i: (i, 0))
    per_b = pl.BlockSpec((1, 1, d), lambda i: ((i * tm) // seq, 0, 0))
    return pl.pallas_call(
        _outproj_kernel,
        grid=(n // tm,),
        in_specs=[row(M_WIDTH), row(A_WIDTH), row(A_WIDTH), row(A_WIDTH),
                  pl.BlockSpec((tm, LANES), lambda i: (i, Z_AG // LANES)),
                  pl.BlockSpec((3, LANES, A_WIDTH), lambda i: (0, 0, 0)),
                  row(d),
                  pl.BlockSpec((M_WIDTH + A_WIDTH, d), lambda i: (0, 0)),
                  per_b,
                  pl.BlockSpec((1, d), lambda i: (0, 0)),
                  per_b, per_b],
        out_specs=[row(d), row(d)],
        out_shape=[jax.ShapeDtypeStruct((n, d), F32)] * 2,
        compiler_params=_cparams("parallel"),
        name="out_projection",
    )(hm2, oc2, os2, ow2, z2, _gate_expand_table(), x2, w_out.astype(BF16), gate1, norm2_g.reshape(1, d),
      scale2, shift2)


def _peer_route_kernel(h2_ref, wqt_ref, sk_ref, eidx_ref, gate_ref):
    tm = h2_ref.shape[0]
    nt = (((1,), (1,)), ((), ()))
    ninf = -3.4e38
    qt = lax.dot_general(wqt_ref[...], h2_ref[...].astype(BF16), nt, preferred_element_type=F32)
    k = P_TOPK
    rowk = lax.broadcasted_iota(jnp.int32, (k, tm), 0)

    def top_rows(sc, payload):
        nrow = sc.shape[0]
        rows = lax.broadcasted_iota(jnp.int32, (nrow, tm), 0)

        def body(r, carry):
            sc, vals, pay = carry
            mx = jnp.max(sc, axis=0, keepdims=True)
            idx = jnp.min(jnp.where(sc == mx, rows, nrow), axis=0, keepdims=True)
            pick = rows == idx
            got = idx if payload is None else jnp.sum(jnp.where(pick, payload, 0), axis=0, keepdims=True)
            vals = jnp.where(rowk == r, mx, vals)
            pay = jnp.where(rowk == r, got, pay)
            return jnp.where(pick, ninf, sc), vals, pay

        _, vals, pay = lax.fori_loop(0, k, body, (sc, jnp.zeros((k, tm), F32), jnp.zeros((k, tm), jnp.int32)))
        return vals, pay

    s1, i1 = top_rows(jnp.dot(sk_ref[0], qt[0:P_HALF], preferred_element_type=F32, precision=HIGHEST), None)
    s2, i2 = top_rows(jnp.dot(sk_ref[1], qt[P_HALF:], preferred_element_type=F32, precision=HIGHEST), None)
    row8 = lax.broadcasted_iota(jnp.int32, (8, tm), 0)
    cand = [s1[0:1] + s2]
    cidx = [i1[0:1] * P_NKEYS + i2]
    for a in range(1, 8):
        keep = row8 < (k // (a + 1))
        cand.append(jnp.where(keep, s1[a:a + 1] + s2[0:8], ninf))
        cidx.append(jnp.where(keep, i1[a:a + 1] * P_NKEYS + i2[0:8], 0))
    cand.append(s1[8:k] + s2[0:1])
    cidx.append(i1[8:k] * P_NKEYS + i2[0:1])
    top, e = top_rows(jnp.concatenate(cand, axis=0), jnp.concatenate(cidx, axis=0))
    ex = jnp.exp(top - top[0:1])
    eidx_ref[0] = e
    gate_ref[0] = ex / jnp.sum(ex, axis=0, keepdims=True)


def _peer_route(h2, wq, subkeys, seq):
    n, d = h2.shape
    tm = min(256, seq)
    wqt = wq.T.astype(BF16)
    out = pl.BlockSpec((1, P_TOPK, tm), lambda i, h: (h, 0, i))
    return pl.pallas_call(
        _peer_route_kernel,
        grid=(n // tm, P_HEADS),
        in_specs=[pl.BlockSpec((tm, d), lambda i, h: (i, 0)),
                  pl.BlockSpec((P_QDIM, d), lambda i, h: (h, 0)),
                  pl.BlockSpec((2, P_NKEYS, P_HALF), lambda i, h: (0, 0, 0))],
        out_specs=[out, out],
        out_shape=[jax.ShapeDtypeStruct((P_HEADS, P_TOPK, n), jnp.int32),
                   jax.ShapeDtypeStruct((P_HEADS, P_TOPK, n), F32)],
        compiler_params=_cparams("parallel", "arbitrary"),
        name="peer_route",
    )(h2, wqt, subkeys)


PEER_TB = 8
PEER_SEL = P_HEADS * P_TOPK


PEER_HALF = PEER_TB // 2


def _peer_expert_kernel(eidx_ref, enext_ref, gt_ref, h2_ref, x1_ref, g2_ref, tab_ref, o_ref, buf, sem):
    i = pl.program_id(0)
    d = h2_ref.shape[1]
    rows = PEER_HALF * PEER_SEL

    def issue(idx_ref, tok, dst_slot, t):
        for j in range(PEER_SEL):
            e = idx_ref[tok, j]
            pltpu.make_async_copy(tab_ref.at[pl.ds(e, 1), :],
                                  buf.at[dst_slot, pl.ds(t * PEER_SEL + j, 1), :],
                                  sem.at[dst_slot]).start(priority=j % 2)

    def wait(s):
        pltpu.make_async_copy(tab_ref.at[pl.ds(0, rows), :], buf.at[s], sem.at[s]).wait()

    def mix(s, t, tok):
        w = buf[s, t * PEER_SEL:(t + 1) * PEER_SEL, :]
        a = jnp.sum(w[:, :d] * h2_ref[tok:tok + 1, :], axis=1, keepdims=True)
        wg = gt_ref[0][:, tok:tok + 1] * _gelu_tanh(a)
        y = jnp.sum(w[:, d:] * wg, axis=0, keepdims=True)
        o_ref[tok:tok + 1, :] = x1_ref[tok:tok + 1, :] + g2_ref[0] * y

    @pl.when(i == 0)
    def _():
        for t in range(PEER_HALF):
            issue(eidx_ref, t, 0, t)

    wait(0)
    for t in range(PEER_HALF):
        issue(eidx_ref, PEER_HALF + t, 1, t)
        mix(0, t, t)
    wait(1)
    for t in range(PEER_HALF):
        issue(enext_ref, t, 0, t)
        mix(1, t, PEER_HALF + t)

    @pl.when(i == pl.num_programs(0) - 1)
    def _():
        wait(0)


def _peer_experts(eidx_t, gate_t, h2, x1, gate2, table, seq):
    n, d = h2.shape
    tb = PEER_TB
    steps = n // tb
    eidx = eidx_t.reshape(PEER_SEL, n).T
    gate_t = gate_t.reshape(PEER_SEL, steps, tb).transpose(1, 0, 2)
    row = pl.BlockSpec((tb, d), lambda i: (i, 0))
    return pl.pallas_call(
        _peer_expert_kernel,
        grid=(steps,),
        in_specs=[pl.BlockSpec((tb, PEER_SEL), lambda i: (i, 0), memory_space=pltpu.SMEM),
                  pl.BlockSpec((tb, PEER_SEL), lambda i: (jnp.minimum(i + 1, steps - 1), 0),
                               memory_space=pltpu.SMEM),
                  pl.BlockSpec((1, PEER_SEL, tb), lambda i: (i, 0, 0)),
                  row, row,
                  pl.BlockSpec((1, 1, d), lambda i: ((i * tb) // seq, 0, 0)),
                  pl.BlockSpec(memory_space=pl.ANY)],
        out_specs=row,
        out_shape=jax.ShapeDtypeStruct((n, d), F32),
        scratch_shapes=[pltpu.VMEM((2, PEER_HALF * PEER_SEL, 2 * d), F32), pltpu.SemaphoreType.DMA((2,))],
        compiler_params=_cparams("arbitrary"),
        name="peer_experts",
    )(eidx, eidx, gate_t, h2, x1, gate2, table)


def kernel(x, c, w_mod, b_mod, norm1_g, norm2_g, w_in, conv_qk, b_igate, b_fgate, mlstm_norm_g, qn_g, kn_g,
           cmp_pos_k, cmp_pos_v, cmp_k_w1, cmp_k_w2, cmp_v_w1, cmp_v_w2, w_out, peer_wq, peer_subkeys,
           peer_u, peer_v):
    b_, s_, d = x.shape
    n = b_ * s_
    for l in range(w_mod.shape[0]):
        mod = _modulation(c, w_mod[l], b_mod[l]).reshape(b_, 6, 1, d)
        shift1, scale1, gate1, shift2, scale2, gate2 = (mod[:, i] for i in range(6))
        x2 = x.reshape(n, d)
        z2 = _in_projection(x2, norm1_g[l], scale1, shift1, _pad_in_weights(w_in[l]), s_)
        z3 = z2.reshape(b_, s_, Z_COLS)
        gate_bias = jnp.concatenate([b_igate[l], b_fgate[l], jnp.zeros((LANES - 2 * M_HEADS,), F32)])
        hm = _mlstm(z3, conv_qk[l], gate_bias.reshape(1, LANES), mlstm_norm_g[l])
        qn, qa, kc, vc, ks, vs, kw, vw = _nsa_prep(z3, qn_g[l], kn_g[l])
        k_cmp = _compress(kc, cmp_pos_k[l], cmp_k_w1[l], cmp_k_w2[l], kn_g[l, 0], True)
        v_cmp = _compress(vc, cmp_pos_v[l], cmp_v_w1[l], cmp_v_w2[l], kn_g[l, 0], False)
        o_cmp, sel = _nsa_cmp(qn, k_cmp, v_cmp)
        o_win, o_sel = _nsa_attn(qn, qa, kw, vw, ks, vs, sel)
        x1, h2 = _out_projection(hm.reshape(n, M_WIDTH), o_cmp.reshape(n, A_WIDTH), o_sel.reshape(n, A_WIDTH),
                                 o_win.reshape(n, A_WIDTH), z2, x2, w_out[l], gate1, norm2_g[l], scale2, shift2, s_)
        eidx, gate = _peer_route(h2, peer_wq[l], peer_subkeys[l], s_)
        table = jnp.concatenate([peer_u[l], peer_v[l]], axis=1)
        x = _peer_experts(eidx, gate, h2, x1, gate2, table, s_).reshape(b_, s_, d)
    return x
```

```python
import functools

import numpy as np
import jax
import jax.numpy as jnp
from jax import lax
from jax.experimental import pallas as pl
from jax.experimental.pallas import tpu as pltpu

F32 = jnp.float32
BF16 = jnp.bfloat16
HIGHEST = lax.Precision.HIGHEST

EPS = 1e-6
D_MODEL = 2048
M_HEADS = 4
M_HEAD_DIM = 256
M_WIDTH = M_HEADS * M_HEAD_DIM
M_CHUNK = 256
CONV_W = 4
A_GROUPS = 4
A_HPG = 4
A_HEADS = A_GROUPS * A_HPG
A_HEAD_DIM = 64
A_WIDTH = A_HEADS * A_HEAD_DIM
CMP_BLOCK = 32
CMP_STRIDE = 16
CMP_HIDDEN = 256
SEL_BLOCK = 64
SEL_TOP = 16
WINDOW = 512
Q_BLOCK = 128
FORCE_SCORE = 1e4
P_HEADS = 8
P_NKEYS = 128
P_TOPK = 16
P_QDIM = 256
P_HALF = 128
MV_OFF = 2 * M_WIDTH
MO_OFF = 3 * M_WIDTH
MIF_OFF = 4 * M_WIDTH
AQ_OFF = MIF_OFF + 2 * M_HEADS
AKV_OFF = AQ_OFF + A_WIDTH
AG_OFF = AKV_OFF + 6 * A_GROUPS * A_HEAD_DIM
IN_COLS = AG_OFF + 3 * A_HEADS
Z_M = 0
Z_AQ = 4096
Z_KV = 5120
Z_IF = 6656
Z_AG = 6784
Z_COLS = 6912
LANES = 128
NEG = -1e30
VMEM_LIMIT = 56 * 1024 * 1024


def _cparams(*sem):
    return pltpu.CompilerParams(dimension_semantics=sem, vmem_limit_bytes=VMEM_LIMIT)


def _gelu_tanh(x):
    return 0.5 * x * (1.0 + jnp.tanh(np.sqrt(2.0 / np.pi).astype(np.float32) * (x + 0.044715 * (x * x * x))))


def _mod_kernel(c_ref, w_ref, b_ref, o_ref):
    c = c_ref[...]
    s = c * jax.nn.sigmoid(c)
    o_ref[...] = jnp.dot(s, w_ref[...], preferred_element_type=F32, precision=HIGHEST) + b_ref[...]


def _modulation(c, w_mod, b_mod):
    b_, d = c.shape
    n = w_mod.shape[1]
    tn = 1024
    cp = jnp.pad(c, ((0, 8 - b_), (0, 0)))
    out = pl.pallas_call(
        _mod_kernel,
        grid=(n // tn,),
        in_specs=[pl.BlockSpec((8, d), lambda j: (0, 0)),
                  pl.BlockSpec((d, tn), lambda j: (0, j)),
                  pl.BlockSpec((1, tn), lambda j: (0, j))],
        out_specs=pl.BlockSpec((8, tn), lambda j: (0, j)),
        out_shape=jax.ShapeDtypeStruct((8, n), F32),
        compiler_params=_cparams("parallel"),
        name="modulation",
    )(cp, w_mod, b_mod.reshape(1, n))
    return out[:b_]


def _inproj_kernel(x_ref, g_ref, sc_ref, sh_ref, w_ref, o_ref, h_scr):
    @pl.when(pl.program_id(1) == 0)
    def _():
        x = x_ref[...]
        r = x * lax.rsqrt(jnp.mean(x * x, axis=-1, keepdims=True) + EPS)
        h = (r * g_ref[...]) * (1.0 + sc_ref[0]) + sh_ref[0]
        h_scr[...] = h.astype(BF16)

    o_ref[...] = jnp.dot(h_scr[...], w_ref[...], preferred_element_type=F32)


def _in_projection(x2, g1, scale1, shift1, w_pad, seq):
    n, d = x2.shape
    zc = w_pad.shape[1]
    tm = min(512, seq)
    tn = 768
    return pl.pallas_call(
        _inproj_kernel,
        grid=(n // tm, zc // tn),
        in_specs=[pl.BlockSpec((tm, d), lambda i, j: (i, 0)),
                  pl.BlockSpec((1, d), lambda i, j: (0, 0)),
                  pl.BlockSpec((1, 1, d), lambda i, j: ((i * tm) // seq, 0, 0)),
                  pl.BlockSpec((1, 1, d), lambda i, j: ((i * tm) // seq, 0, 0)),
                  pl.BlockSpec((d, tn), lambda i, j: (0, j))],
        out_specs=pl.BlockSpec((tm, tn), lambda i, j: (i, j)),
        out_shape=jax.ShapeDtypeStruct((n, zc), F32),
        scratch_shapes=[pltpu.VMEM((tm, d), BF16)],
        compiler_params=_cparams("parallel", "arbitrary"),
        name="in_projection",
    )(x2, g1.reshape(1, d), scale1, shift1, w_pad)


def _pad_in_weights(w_in):
    d = w_in.shape[0]
    z = lambda k: jnp.zeros((d, k), w_in.dtype)
    w = jnp.concatenate([
        w_in[:, :MIF_OFF],
        w_in[:, AQ_OFF:AKV_OFF],
        w_in[:, AKV_OFF:AG_OFF],
        w_in[:, MIF_OFF:AQ_OFF], z(LANES - 2 * M_HEADS),
        w_in[:, AG_OFF:IN_COLS], z(LANES - 3 * A_HEADS),
    ], axis=1)
    return w.astype(BF16)


def _mlstm_kernel(q_ref, k_ref, v_ref, o_ref, if_ref, cq_ref, ck_ref, bias_ref, g_ref, out_ref,
                  c_scr, n_scr, m_scr, qt_scr, kt_scr):
    L, d = q_ref.shape[1], q_ref.shape[2]
    h = pl.program_id(1)

    @pl.when(pl.program_id(2) == 0)
    def _():
        c_scr[...] = jnp.zeros_like(c_scr)
        n_scr[...] = jnp.zeros_like(n_scr)
        m_scr[...] = jnp.zeros_like(m_scr)
        qt_scr[...] = jnp.zeros_like(qt_scr)
        kt_scr[...] = jnp.zeros_like(kt_scr)

    def conv_silu(raw, tail_scr, w_ref):
        ext = jnp.concatenate([tail_scr[...], raw], axis=0)
        acc = jnp.zeros((L, d), F32)
        for j in range(CONV_W):
            off = 8 - (CONV_W - 1) + j
            acc = acc + w_ref[j:j + 1, :] * ext[off:off + L, :]
        tail_scr[...] = raw[L - 8:, :]
        return acc * jax.nn.sigmoid(acc)

    qc = conv_silu(q_ref[0], qt_scr, cq_ref)
    kc = conv_silu(k_ref[0], kt_scr, ck_ref) * (d ** -0.5)
    vc = v_ref[0]

    gt = if_ref[0] + bias_ref[...]
    lane = lax.broadcasted_iota(jnp.int32, (L, LANES), 1)
    logsig = jnp.minimum(gt, 0.0) - jnp.log(1.0 + jnp.exp(-jnp.abs(gt)))
    gt = jnp.where(lane < M_HEADS, gt, logsig)
    rows = lax.broadcasted_iota(jnp.int32, (L, L), 0)
    cols = lax.broadcasted_iota(jnp.int32, (L, L), 1)
    causal = rows >= cols
    tril = jnp.where(causal, 1.0, 0.0).astype(F32)
    csum = jnp.dot(tril, gt, preferred_element_type=F32, precision=HIGHEST)
    gt_t = gt.T
    csum_t = csum.T
    sub = lax.broadcasted_iota(jnp.int32, (LANES, L), 0)
    ig_col = jnp.sum(jnp.where(lane == h, gt, 0.0), axis=1, keepdims=True)
    b_col = jnp.sum(jnp.where(lane == h + M_HEADS, csum, 0.0), axis=1, keepdims=True)
    ig_row = jnp.sum(jnp.where(sub == h, gt_t, 0.0), axis=0, keepdims=True)
    b_row = jnp.sum(jnp.where(sub == h + M_HEADS, csum_t, 0.0), axis=0, keepdims=True)

    m_st = m_scr[0:1, 0:1]
    dmat = jnp.where(causal, b_col - b_row + ig_row, NEG)
    m_inter = b_col + m_st
    m_t = jnp.maximum(m_inter, jnp.max(dmat, axis=1, keepdims=True))
    qb = qc.astype(BF16)
    kb = kc.astype(BF16)
    vb = vc.astype(BF16)
    qk = lax.dot_general(qb, kb, (((1,), (1,)), ((), ())), preferred_element_type=F32)
    w = jnp.exp(dmat - m_t) * qk
    a_inter = jnp.exp(m_inter - m_t)
    c_st = c_scr[...]
    n_st = n_scr[...]
    num = (jnp.dot(w.astype(BF16), vb, preferred_element_type=F32)
           + a_inter * jnp.dot(qb, c_st.astype(BF16), preferred_element_type=F32))
    den = jnp.sum(w, axis=1, keepdims=True) + a_inter * jnp.sum(qc * n_st, axis=1, keepdims=True)
    hout = num / jnp.maximum(jnp.abs(den), jnp.exp(-m_t))

    b_last = b_col[L - 1:L, :]
    a_s = b_last - b_col + ig_col
    m_new = jnp.maximum(b_last + m_st, jnp.max(a_s, axis=0, keepdims=True))
    w_s = jnp.exp(a_s - m_new)
    decay = jnp.exp(b_last + m_st - m_new)
    kw = kc * w_s
    c_scr[...] = decay * c_st + lax.dot_general(kw.astype(BF16), vb, (((0,), (0,)), ((), ())),
                                                preferred_element_type=F32)
    n_scr[...] = decay * n_st + jnp.sum(kw, axis=0, keepdims=True)
    m_scr[...] = jnp.broadcast_to(m_new, m_scr.shape)

    r = hout * lax.rsqrt(jnp.mean(hout * hout, axis=-1, keepdims=True) + EPS)
    out_ref[0] = (r * g_ref[...]) * jax.nn.sigmoid(o_ref[0])


def _mlstm(z3, conv_qk, gate_bias, norm_g):
    b_, s_, _ = z3.shape
    L = min(M_CHUNK, s_)
    d = M_HEAD_DIM
    H = M_HEADS
    blk = lambda off: pl.BlockSpec((1, L, d), lambda b, h, c: (b, c, off + h))
    return pl.pallas_call(
        _mlstm_kernel,
        grid=(b_, H, s_ // L),
        in_specs=[blk(0), blk(H), blk(2 * H), blk(3 * H),
                  pl.BlockSpec((1, L, LANES), lambda b, h, c: (b, c, Z_IF // LANES)),
                  pl.BlockSpec((CONV_W, d), lambda b, h, c: (0, h)),
                  pl.BlockSpec((CONV_W, d), lambda b, h, c: (0, H + h)),
                  pl.BlockSpec((1, LANES), lambda b, h, c: (0, 0)),
                  pl.BlockSpec((1, d), lambda b, h, c: (0, h))],
        out_specs=pl.BlockSpec((1, L, d), lambda b, h, c: (b, c, h)),
        out_shape=jax.ShapeDtypeStruct((b_, s_, M_WIDTH), F32),
        scratch_shapes=[pltpu.VMEM((d, d), F32), pltpu.VMEM((1, d), F32), pltpu.VMEM((8, LANES), F32),
                        pltpu.VMEM((8, d), F32), pltpu.VMEM((8, d), F32)],
        compiler_params=_cparams("parallel", "parallel", "arbitrary"),
        name="mlstm",
    )(z3, z3, z3, z3, z3, conv_qk, conv_qk, gate_bias, norm_g.reshape(1, M_WIDTH))


SEL_TILE = 512
LOG2E = float(np.log2(np.e))


def _nsa_prep_kernel(aq_ref, kc_ref, vc_ref, ks_ref, vs_ref, kw_ref, vw_ref, qg_ref, kg_ref, slc_ref,
                     qn_o, qa_o, kc_o, vc_o, ks_o, vs_o, kw_o, vw_o):
    hd = A_HEAD_DIM
    ts = aq_ref.shape[1]

    def norm(xh, g):
        return (xh * lax.rsqrt(jnp.mean(xh * xh, axis=-1, keepdims=True) + EPS)) * g

    aq = aq_ref[0]
    qg = qg_ref[...]
    for h in range(A_HEADS):
        qh = norm(aq[:, h * hd:(h + 1) * hd], qg) * (hd ** -0.5)
        qn_o[0, h // A_HPG, h % A_HPG] = qh.astype(BF16)
        slope_cols = jnp.broadcast_to(slc_ref[h:h + 1, :], (ts, hd))
        qa_o[0, h // A_HPG, h % A_HPG] = jnp.concatenate([qh * LOG2E, slope_cols], axis=1).astype(BF16)
    pos = pl.program_id(1) * ts + lax.broadcasted_iota(jnp.int32, (ts, hd), 0)
    lane = lax.broadcasted_iota(jnp.int32, (ts, hd), 1)
    c = pos & (SEL_TILE - 1)
    kind = lambda r: (lane == r) | (lane == r + 3) | (lane == r + 6)
    pos_cols = jnp.where(kind(0), c & 255, jnp.where(kind(1), c & 256, jnp.where(kind(2), pos - c, 0))).astype(F32)
    kc, vc, ks, vs, kw, vw = kc_ref[0], vc_ref[0], ks_ref[0], vs_ref[0], kw_ref[0], vw_ref[0]
    for g in range(A_GROUPS):
        sl = slice(g * hd, (g + 1) * hd)
        kc_o[0, g] = kc[:, sl]
        vc_o[0, g] = vc[:, sl]
        ks_o[0, g] = jnp.concatenate([norm(ks[:, sl], kg_ref[1:2, :]), pos_cols], axis=1).astype(BF16)
        vs_o[0, g] = vs[:, sl].astype(BF16)
        kw_o[0, g] = norm(kw[:, sl], kg_ref[2:3, :]).astype(BF16)
        vw_o[0, g] = vw[:, sl].astype(BF16)


def _slope_cols_table():
    slopes = 2.0 ** (-8.0 * (np.arange(A_HEADS) + 1) / A_HEADS) * np.log2(np.e)
    tab = np.zeros((A_HEADS, A_HEAD_DIM), np.float32)
    rem = slopes.astype(np.float64)
    for i in range(3):
        part = rem.astype(np.float32).astype(BF16).astype(np.float64)
        tab[:, 3 * i:3 * i + 3] = part[:, None]
        rem = rem - part
    return jnp.asarray(tab)


def _nsa_prep(z3, qn_g, kn_g):
    b_, s_, _ = z3.shape
    ts = min(512, s_)
    G, hd = A_GROUPS, A_HEAD_DIM
    kvw = G * hd
    kv_in = lambda i: pl.BlockSpec((1, ts, kvw), lambda b, t: (b, t, Z_KV // kvw + i))
    kv_out = lambda w: pl.BlockSpec((1, G, ts, w), lambda b, t: (b, 0, t, 0))
    q_out = lambda w: pl.BlockSpec((1, G, A_HPG, ts, w), lambda b, t: (b, 0, 0, t, 0))
    kv_shape = lambda dt, w=hd: jax.ShapeDtypeStruct((b_, G, s_, w), dt)
    return pl.pallas_call(
        _nsa_prep_kernel,
        grid=(b_, s_ // ts),
        in_specs=[pl.BlockSpec((1, ts, A_WIDTH), lambda b, t: (b, t, Z_AQ // A_WIDTH))]
                 + [kv_in(i) for i in range(6)]
                 + [pl.BlockSpec((1, hd), lambda b, t: (0, 0)), pl.BlockSpec((3, hd), lambda b, t: (0, 0)),
                    pl.BlockSpec((A_HEADS, hd), lambda b, t: (0, 0))],
        out_specs=[q_out(hd), q_out(2 * hd), kv_out(hd), kv_out(hd), kv_out(2 * hd), kv_out(hd), kv_out(hd),
                   kv_out(hd)],
        out_shape=[jax.ShapeDtypeStruct((b_, G, A_HPG, s_, hd), BF16),
                   jax.ShapeDtypeStruct((b_, G, A_HPG, s_, 2 * hd), BF16),
                   kv_shape(F32), kv_shape(F32), kv_shape(BF16, 2 * hd), kv_shape(BF16), kv_shape(BF16),
                   kv_shape(BF16)],
        compiler_params=_cparams("parallel", "parallel"),
        name="nsa_prep",
    )(z3, z3, z3, z3, z3, z3, z3, qn_g.reshape(1, hd), kn_g, _slope_cols_table())


def _compress_kernel(r_ref, pos_ref, w1a_ref, w1b_ref, w2_ref, g_ref, o_ref, *, do_norm):
    r = r_ref[0, 0]
    nr = r.shape[0]
    u = jnp.dot((r + pos_ref[0:1, :]).astype(BF16), w1a_ref[...], preferred_element_type=F32)
    v = jnp.dot((r + pos_ref[1:2, :]).astype(BF16), w1b_ref[...], preferred_element_type=F32)
    pre = u + pltpu.roll(v, nr - 1, 0)
    out = jnp.dot(_gelu_tanh(pre).astype(BF16), w2_ref[...], preferred_element_type=F32)
    if do_norm:
        out = (out * lax.rsqrt(jnp.mean(out * out, axis=-1, keepdims=True) + EPS)) * g_ref[...]
    o_ref[0, 0] = out.astype(BF16)


def _compress(a, pos, w1, w2, g, do_norm):
    b_, G, s_, hd = a.shape
    nr = s_ // CMP_STRIDE
    half = CMP_STRIDE * hd
    r = a.reshape(b_, G, nr, half)
    w1b16 = w1.astype(BF16)
    return pl.pallas_call(
        functools.partial(_compress_kernel, do_norm=do_norm),
        grid=(b_, G),
        in_specs=[pl.BlockSpec((1, 1, nr, half), lambda b, g: (b, g, 0, 0)),
                  pl.BlockSpec((2, half), lambda b, g: (0, 0)),
                  pl.BlockSpec((half, CMP_HIDDEN), lambda b, g: (0, 0)),
                  pl.BlockSpec((half, CMP_HIDDEN), lambda b, g: (1, 0)),
                  pl.BlockSpec((CMP_HIDDEN, hd), lambda b, g: (0, 0)),
                  pl.BlockSpec((1, hd), lambda b, g: (0, 0))],
        out_specs=pl.BlockSpec((1, 1, nr, hd), lambda b, g: (b, g, 0, 0)),
        out_shape=jax.ShapeDtypeStruct((b_, G, nr, hd), BF16),
        compiler_params=_cparams("parallel", "parallel"),
        name="nsa_compress_norm" if do_norm else "nsa_compress",
    )(r, pos.reshape(2, half), w1b16, w1b16, w2.astype(BF16), g.reshape(1, hd))


def _nsa_cmp_kernel(q_ref, kc_ref, vc_ref, ov_ref, slope_ref, ocmp_ref, sel_ref, *, n_cmp, n_blk, n_sel):
    T, hd = q_ref.shape[3], q_ref.shape[4]
    R = A_HPG * T
    nc = kc_ref.shape[2]
    q0 = pl.program_id(2) * T
    q = q_ref[0, 0].reshape(R, hd)
    s = lax.dot_general(q, kc_ref[0, 0], (((1,), (1,)), ((), ())), preferred_element_type=F32)
    row = lax.broadcasted_iota(jnp.int32, (R, nc), 0)
    ci = lax.broadcasted_iota(jnp.int32, (R, nc), 1)
    t = q0 + (row & (T - 1))
    disti = t - (ci * CMP_STRIDE + (CMP_BLOCK - 1))
    valid = (disti >= 0) & (ci < n_cmp)
    slope = slope_ref[0][:, 0:1]
    s = jnp.where(valid, s - slope * disti.astype(F32), NEG)
    m = jnp.max(s, axis=1, keepdims=True)
    e = jnp.where(valid, jnp.exp(s - m), 0.0)
    p = e / jnp.maximum(jnp.sum(e, axis=1, keepdims=True), 1e-30)
    oc = jnp.dot(p.astype(BF16), vc_ref[0, 0], preferred_element_type=F32)
    for h in range(A_HPG):
        ocmp_ref[0, :, h * hd:(h + 1) * hd] = oc[h * T:(h + 1) * T, :]
    ps = p[0:T] + p[T:2 * T] + p[2 * T:3 * T] + p[3 * T:4 * T]
    hi = ps.astype(BF16)
    lo = (ps - hi.astype(F32)).astype(BF16)
    ov = ov_ref[...]
    imp = jnp.dot(hi, ov, preferred_element_type=F32) + jnp.dot(lo, ov, preferred_element_type=F32)
    imp = imp.T
    blk = lax.broadcasted_iota(jnp.int32, (LANES, T), 0)
    cur = (q0 + lax.broadcasted_iota(jnp.int32, (LANES, T), 1)) // SEL_BLOCK
    forced = (blk == 0) | (blk == cur) | (blk == cur - 1)
    imp = jnp.where(forced, FORCE_SCORE, jnp.where(blk <= cur, imp, -FORCE_SCORE))
    imp = jnp.where(blk < n_blk, imp, -3e38)
    sel = jnp.zeros((LANES, T), F32)
    for _ in range(n_sel):
        mx = jnp.max(imp, axis=0, keepdims=True)
        idx = jnp.min(jnp.where(imp == mx, blk, LANES), axis=0, keepdims=True)
        pick = blk == idx
        sel = jnp.where(pick, 1.0, sel)
        imp = jnp.where(pick, -3.4e38, imp)
    sel_ref[0, 0] = sel.T.astype(BF16)


def _slope_table():
    slopes = (2.0 ** (-8.0 * (np.arange(A_HEADS) + 1) / A_HEADS)).astype(np.float32).reshape(A_GROUPS, A_HPG)
    tab = np.repeat(slopes, Q_BLOCK, axis=1)
    return jnp.asarray(np.broadcast_to(tab[:, :, None], (A_GROUPS, A_HPG * Q_BLOCK, LANES)).copy())


def _overlap_table(s_, nc):
    n_cmp = (s_ - CMP_BLOCK) // CMP_STRIDE + 1
    n_blk = s_ // SEL_BLOCK
    starts = np.arange(nc)[:, None] * CMP_STRIDE
    blk_starts = np.arange(LANES)[None, :] * SEL_BLOCK
    ov = (starts < blk_starts + SEL_BLOCK) & (starts + CMP_BLOCK > blk_starts)
    ov &= (np.arange(nc)[:, None] < n_cmp) & (np.arange(LANES)[None, :] < n_blk)
    return jnp.asarray(ov.astype(np.float32)).astype(BF16)


def _nsa_cmp(qn, k_cmp, v_cmp):
    b_, G, hpg, s_, hd = qn.shape
    T = Q_BLOCK
    nc = k_cmp.shape[2]
    n_cmp = (s_ - CMP_BLOCK) // CMP_STRIDE + 1
    n_blk = s_ // SEL_BLOCK
    kern = functools.partial(_nsa_cmp_kernel, n_cmp=n_cmp, n_blk=n_blk, n_sel=min(SEL_TOP, n_blk))
    return pl.pallas_call(
        kern,
        grid=(b_, G, s_ // T),
        in_specs=[pl.BlockSpec((1, 1, hpg, T, hd), lambda b, g, i: (b, g, 0, i, 0)),
                  pl.BlockSpec((1, 1, nc, hd), lambda b, g, i: (b, g, 0, 0)),
                  pl.BlockSpec((1, 1, nc, hd), lambda b, g, i: (b, g, 0, 0)),
                  pl.BlockSpec((nc, LANES), lambda b, g, i: (0, 0)),
                  pl.BlockSpec((1, hpg * T, LANES), lambda b, g, i: (g, 0, 0))],
        out_specs=[pl.BlockSpec((1, T, hpg * hd), lambda b, g, i: (b, i, g)),
                   pl.BlockSpec((1, 1, T, LANES), lambda b, g, i: (b, g, i, 0))],
        out_shape=[jax.ShapeDtypeStruct((b_, s_, A_WIDTH), F32),
                   jax.ShapeDtypeStruct((b_, G, s_, LANES), BF16)],
        compiler_params=_cparams("parallel", "parallel", "parallel"),
        name="nsa_cmp_topk",
    )(qn, k_cmp, v_cmp, _overlap_table(s_, nc), _slope_table())


def _nsa_attn_kernel(q_ref, qa_ref, kw_ref, vw_ref, ks_ref, vs_ref, sel_ref, slope_ref, owin_ref, osel_ref,
                     m_scr, l_scr, acc_scr, *, wk, tk):
    T, hd = q_ref.shape[3], q_ref.shape[4]
    R = A_HPG * T
    qb = pl.program_id(2)
    q0 = qb * T
    q = q_ref[0, 0].reshape(R, hd)
    slope = slope_ref[0][:, 0:1]
    nt = (((1,), (1,)), ((), ()))

    start = pl.multiple_of(jnp.maximum(q0 + T - wk, 0), T)
    kw = kw_ref[0, 0, pl.ds(start, wk), :]
    vw = vw_ref[0, 0, pl.ds(start, wk), :]
    s = lax.dot_general(q, kw, nt, preferred_element_type=F32)
    t = q0 + (lax.broadcasted_iota(jnp.int32, (R, wk), 0) & (T - 1))
    disti = t - (start + lax.broadcasted_iota(jnp.int32, (R, wk), 1))
    mask = (disti >= 0) & (disti < WINDOW)
    s = jnp.where(mask, s - slope * disti.astype(F32), NEG)
    m = jnp.max(s, axis=1, keepdims=True)
    e = jnp.exp(s - m)
    ow = jnp.dot(e.astype(BF16), vw, preferred_element_type=F32) / jnp.sum(e, axis=1, keepdims=True)
    for h in range(A_HPG):
        owin_ref[0, :, h * hd:(h + 1) * hd] = ow[h * T:(h + 1) * T, :]

    qa = qa_ref[0, 0].reshape(R, 2 * hd)
    sel = sel_ref[0, 0]
    sel_bias = ((sel.astype(F32) - 1.0) * 1e30).astype(BF16)
    m_scr[...] = jnp.full_like(m_scr, NEG)
    l_scr[...] = jnp.zeros_like(l_scr)
    acc_scr[...] = jnp.zeros_like(acc_scr)
    bpt = tk // SEL_BLOCK
    lane = lax.broadcasted_iota(jnp.int32, (T, LANES), 1)
    j_diag = q0 // tk

    def tile(j, diagonal):
        kv0 = pl.multiple_of(j * tk, tk)
        bi = lax.broadcasted_iota(jnp.int32, (LANES, tk), 0)
        cc = lax.broadcasted_iota(jnp.int32, (LANES, tk), 1)
        expand = jnp.where(((kv0 + cc) // SEL_BLOCK) == bi, 1.0, 0.0).astype(BF16)
        bias = jnp.dot(sel_bias, expand, preferred_element_type=F32)
        k = ks_ref[0, 0, pl.ds(kv0, tk), :]
        v = vs_ref[0, 0, pl.ds(kv0, tk), :]
        sc = lax.dot_general(qa, k, nt, preferred_element_type=F32) + jnp.concatenate([bias] * A_HPG, axis=0)
        if diagonal:
            tt = q0 + (lax.broadcasted_iota(jnp.int32, (R, tk), 0) & (T - 1))
            sc = jnp.where(tt >= kv0 + lax.broadcasted_iota(jnp.int32, (R, tk), 1), sc, NEG)
        m_old = m_scr[...]
        m_new = jnp.maximum(m_old, jnp.max(sc, axis=1, keepdims=True))
        alpha = jnp.exp2(m_old - m_new)
        p = jnp.exp2(sc - m_new)
        l_scr[...] = alpha * l_scr[...] + jnp.sum(p, axis=1, keepdims=True)
        acc_scr[...] = alpha * acc_scr[...] + jnp.dot(p.astype(BF16), v, preferred_element_type=F32)
        m_scr[...] = m_new

    def body(j, carry):
        in_tile = (lane >= j * bpt) & (lane < (j + 1) * bpt)
        has = jnp.max(jnp.where(in_tile, sel.astype(F32), 0.0)) > 0.5

        @pl.when(jnp.logical_or(has, j == 0))
        def _():
            tile(j, False)

        return carry

    lax.fori_loop(0, j_diag, body, 0)
    tile(j_diag, True)
    os_ = acc_scr[...] / l_scr[...]
    for h in range(A_HPG):
        osel_ref[0, :, h * hd:(h + 1) * hd] = os_[h * T:(h + 1) * T, :]


def _nsa_attn(qn, qa, k_win, v_win, k_sel_aug, v_sel, sel):
    b_, G, hpg, s_, hd = qn.shape
    T = Q_BLOCK
    wk = min(WINDOW + T, s_)
    tk = SEL_TILE
    assert s_ % tk == 0
    full = lambda w: pl.BlockSpec((1, 1, s_, w), lambda b, g, i: (b, g, 0, 0))
    q_blk = lambda w: pl.BlockSpec((1, 1, hpg, T, w), lambda b, g, i: (b, g, 0, i, 0))
    out = pl.BlockSpec((1, T, hpg * hd), lambda b, g, i: (b, i, g))
    return pl.pallas_call(
        functools.partial(_nsa_attn_kernel, wk=wk, tk=tk),
        grid=(b_, G, s_ // T),
        in_specs=[q_blk(hd), q_blk(2 * hd),
                  full(hd), full(hd), full(2 * hd), full(hd),
                  pl.BlockSpec((1, 1, T, LANES), lambda b, g, i: (b, g, i, 0)),
                  pl.BlockSpec((1, hpg * T, LANES), lambda b, g, i: (g, 0, 0))],
        out_specs=[out, out],
        out_shape=[jax.ShapeDtypeStruct((b_, s_, A_WIDTH), F32)] * 2,
        scratch_shapes=[pltpu.VMEM((hpg * T, 1), F32), pltpu.VMEM((hpg * T, 1), F32),
                        pltpu.VMEM((hpg * T, hd), F32)],
        compiler_params=_cparams("parallel", "parallel", "arbitrary"),
        name="nsa_window_selected",
    )(qn, qa, k_win, v_win, k_sel_aug, v_sel, sel, _slope_table())


def _outproj_kernel(hm_ref, oc_ref, os_ref, ow_ref, ag_ref, ex_ref, x_ref, wo_ref, g1_ref, n2_ref, sc2_ref,
                    sh2_ref, x1_ref, h2_ref):
    sg = jax.nn.sigmoid(ag_ref[...])
    hi = sg.astype(BF16)
    lo = (sg - hi.astype(F32)).astype(BF16)
    ha = jnp.zeros(oc_ref.shape, F32)
    for br, o_ref in enumerate((oc_ref, os_ref, ow_ref)):
        ex = ex_ref[br]
        gexp = jnp.dot(hi, ex, preferred_element_type=F32) + jnp.dot(lo, ex, preferred_element_type=F32)
        ha = ha + gexp * o_ref[...]
    mw = hm_ref.shape[1]
    y = (jnp.dot(hm_ref[...].astype(BF16), wo_ref[0:mw, :], preferred_element_type=F32)
         + jnp.dot(ha.astype(BF16), wo_ref[mw:, :], preferred_element_type=F32))
    x1 = x_ref[...] + g1_ref[0] * y
    x1_ref[...] = x1
    r = x1 * lax.rsqrt(jnp.mean(x1 * x1, axis=-1, keepdims=True) + EPS)
    h2_ref[...] = (r * n2_ref[...]) * (1.0 + sc2_ref[0]) + sh2_ref[0]


def _gate_expand_table():
    ex = np.zeros((3, LANES, A_WIDTH), np.float32)
    for hd in range(A_HEADS):
        for br in range(3):
            ex[br, hd * 3 + br, hd * A_HEAD_DIM:(hd + 1) * A_HEAD_DIM] = 1.0
    return jnp.asarray(ex).astype(BF16)


def _out_projection(hm2, oc2, os2, ow2, z2, x2, w_out, gate1, norm2_g, scale2, shift2, seq):
    n, d = x2.shape
    tm = min(256, seq)
    row = lambda w: pl.BlockSpec((tm, w), lambda i: (i, 0))
    per_b = pl.BlockSpec((1, 1, d), lambda i: ((i * tm) // seq, 0, 0))
    return pl.pallas_call(
        _outproj_kernel,
        grid=(n // tm,),
        in_specs=[row(M_WIDTH), row(A_WIDTH), row(A_WIDTH), row(A_WIDTH),
                  pl.BlockSpec((tm, LANES), lambda i: (i, Z_AG // LANES)),
                  pl.BlockSpec((3, LANES, A_WIDTH), lambda i: (0, 0, 0)),
                  row(d),
                  pl.BlockSpec((M_WIDTH + A_WIDTH, d), lambda i: (0, 0)),
                  per_b,
                  pl.BlockSpec((1, d), lambda i: (0, 0)),
                  per_b, per_b],
        out_specs=[row(d), row(d)],
        out_shape=[jax.ShapeDtypeStruct((n, d), F32)] * 2,
        compiler_params=_cparams("parallel"),
        name="out_projection",
    )(hm2, oc2, os2, ow2, z2, _gate_expand_table(), x2, w_out.astype(BF16), gate1, norm2_g.reshape(1, d),
      scale2, shift2)


def _peer_route_kernel(h2_ref, wqt_ref, sk_ref, eidx_ref, gate_ref):
    tm = h2_ref.shape[0]
    nt = (((1,), (1,)), ((), ()))
    ninf = -3.4e38
    qt = lax.dot_general(wqt_ref[...], h2_ref[...].astype(BF16), nt, preferred_element_type=F32)
    k = P_TOPK
    rowk = lax.broadcasted_iota(jnp.int32, (k, tm), 0)

    def top_rows(sc, payload):
        nrow = sc.shape[0]
        rows = lax.broadcasted_iota(jnp.int32, (nrow, tm), 0)

        def body(r, carry):
            sc, vals, pay = carry
            mx = jnp.max(sc, axis=0, keepdims=True)
            idx = jnp.min(jnp.where(sc == mx, rows, nrow), axis=0, keepdims=True)
            pick = rows == idx
            got = idx if payload is None else jnp.sum(jnp.where(pick, payload, 0), axis=0, keepdims=True)
            vals = jnp.where(rowk == r, mx, vals)
            pay = jnp.where(rowk == r, got, pay)
            return jnp.where(pick, ninf, sc), vals, pay

        _, vals, pay = lax.fori_loop(0, k, body, (sc, jnp.zeros((k, tm), F32), jnp.zeros((k, tm), jnp.int32)))
        return vals, pay

    s1, i1 = top_rows(jnp.dot(sk_ref[0], qt[0:P_HALF], preferred_element_type=F32, precision=HIGHEST), None)
    s2, i2 = top_rows(jnp.dot(sk_ref[1], qt[P_HALF:], preferred_element_type=F32, precision=HIGHEST), None)
    row8 = lax.broadcasted_iota(jnp.int32, (8, tm), 0)
    cand = [s1[0:1] + s2]
    cidx = [i1[0:1] * P_NKEYS + i2]
    for a in range(1, 8):
        keep = row8 < (k // (a + 1))
        cand.append(jnp.where(keep, s1[a:a + 1] + s2[0:8], ninf))
        cidx.append(jnp.where(keep, i1[a:a + 1] * P_NKEYS + i2[0:8], 0))
    cand.append(s1[8:k] + s2[0:1])
    cidx.append(i1[8:k] * P_NKEYS + i2[0:1])
    top, e = top_rows(jnp.concatenate(cand, axis=0), jnp.concatenate(cidx, axis=0))
    ex = jnp.exp(top - top[0:1])
    eidx_ref[0] = e
    gate_ref[0] = ex / jnp.sum(ex, axis=0, keepdims=True)


def _peer_route(h2, wq, subkeys, seq):
    n, d = h2.shape
    tm = min(256, seq)
    wqt = wq.T.astype(BF16)
    out = pl.BlockSpec((1, P_TOPK, tm), lambda i, h: (h, 0, i))
    return pl.pallas_call(
        _peer_route_kernel,
        grid=(n // tm, P_HEADS),
        in_specs=[pl.BlockSpec((tm, d), lambda i, h: (i, 0)),
                  pl.BlockSpec((P_QDIM, d), lambda i, h: (h, 0)),
                  pl.BlockSpec((2, P_NKEYS, P_HALF), lambda i, h: (0, 0, 0))],
        out_specs=[out, out],
        out_shape=[jax.ShapeDtypeStruct((P_HEADS, P_TOPK, n), jnp.int32),
                   jax.ShapeDtypeStruct((P_HEADS, P_TOPK, n), F32)],
        compiler_params=_cparams("parallel", "arbitrary"),
        name="peer_route",
    )(h2, wqt, subkeys)


PEER_TB = 8
PEER_SEL = P_HEADS * P_TOPK


PEER_HALF = PEER_TB // 2


def _peer_expert_kernel(eidx_ref, enext_ref, gt_ref, h2_ref, x1_ref, g2_ref, tab_ref, o_ref, buf, sem):
    i = pl.program_id(0)
    d = h2_ref.shape[1]
    rows = PEER_HALF * PEER_SEL

    def issue(idx_ref, tok, dst_slot, t):
        for j in range(PEER_SEL):
            e = idx_ref[tok, j]
            pltpu.make_async_copy(tab_ref.at[e],
                                  buf.at[dst_slot, pl.ds(t * PEER_SEL + j, 1), :],
                                  sem.at[dst_slot]).start()

    def wait(s):
        pltpu.make_async_copy(tab_ref.at[pl.ds(0, rows), 0, :], buf.at[s], sem.at[s]).wait()

    def mix(s, t, tok):
        w = buf[s, t * PEER_SEL:(t + 1) * PEER_SEL, :]
        a = jnp.sum(w[:, :d] * h2_ref[tok:tok + 1, :], axis=1, keepdims=True)
        wg = gt_ref[0][:, tok:tok + 1] * _gelu_tanh(a)
        y = jnp.sum(w[:, d:] * wg, axis=0, keepdims=True)
        o_ref[tok:tok + 1, :] = x1_ref[tok:tok + 1, :] + g2_ref[0] * y

    @pl.when(i == 0)
    def _():
        for t in range(PEER_HALF):
            issue(eidx_ref, t, 0, t)

    wait(0)
    for t in range(PEER_HALF):
        issue(eidx_ref, PEER_HALF + t, 1, t)
        mix(0, t, t)
    wait(1)
    for t in range(PEER_HALF):
        issue(enext_ref, t, 0, t)
        mix(1, t, PEER_HALF + t)

    @pl.when(i == pl.num_programs(0) - 1)
    def _():
        wait(0)


def _peer_experts(eidx_t, gate_t, h2, x1, gate2, table, seq):
    n, d = h2.shape
    tb = PEER_TB
    steps = n // tb
    eidx = eidx_t.reshape(PEER_SEL, n).T
    gate_t = gate_t.reshape(PEER_SEL, steps, tb).transpose(1, 0, 2)
    row = pl.BlockSpec((tb, d), lambda i: (i, 0))
    return pl.pallas_call(
        _peer_expert_kernel,
        grid=(steps,),
        in_specs=[pl.BlockSpec((tb, PEER_SEL), lambda i: (i, 0), memory_space=pltpu.SMEM),
                  pl.BlockSpec((tb, PEER_SEL), lambda i: (jnp.minimum(i + 1, steps - 1), 0),
                               memory_space=pltpu.SMEM),
                  pl.BlockSpec((1, PEER_SEL, tb), lambda i: (i, 0, 0)),
                  row, row,
                  pl.BlockSpec((1, 1, d), lambda i: ((i * tb) // seq, 0, 0)),
                  pl.BlockSpec(memory_space=pl.ANY)],
        out_specs=row,
        out_shape=jax.ShapeDtypeStruct((n, d), F32),
        scratch_shapes=[pltpu.VMEM((2, PEER_HALF * PEER_SEL, 2 * d), F32), pltpu.SemaphoreType.DMA((2,))],
        compiler_params=_cparams("arbitrary"),
        name="peer_experts",
    )(eidx, eidx, gate_t, h2, x1, gate2, table)


def kernel(x, c, w_mod, b_mod, norm1_g, norm2_g, w_in, conv_qk, b_igate, b_fgate, mlstm_norm_g, qn_g, kn_g,
           cmp_pos_k, cmp_pos_v, cmp_k_w1, cmp_k_w2, cmp_v_w1, cmp_v_w2, w_out, peer_wq, peer_subkeys,
           peer_u, peer_v):
    b_, s_, d = x.shape
    n = b_ * s_
    for l in range(w_mod.shape[0]):
        mod = _modulation(c, w_mod[l], b_mod[l]).reshape(b_, 6, 1, d)
        shift1, scale1, gate1, shift2, scale2, gate2 = (mod[:, i] for i in range(6))
        x2 = x.reshape(n, d)
        z2 = _in_projection(x2, norm1_g[l], scale1, shift1, _pad_in_weights(w_in[l]), s_)
        z3 = z2.reshape(b_, s_, Z_COLS)
        gate_bias = jnp.concatenate([b_igate[l], b_fgate[l], jnp.zeros((LANES - 2 * M_HEADS,), F32)])
        hm = _mlstm(z3, conv_qk[l], gate_bias.reshape(1, LANES), mlstm_norm_g[l])
        qn, qa, kc, vc, ks, vs, kw, vw = _nsa_prep(z3, qn_g[l], kn_g[l])
        k_cmp = _compress(kc, cmp_pos_k[l], cmp_k_w1[l], cmp_k_w2[l], kn_g[l, 0], True)
        v_cmp = _compress(vc, cmp_pos_v[l], cmp_v_w1[l], cmp_v_w2[l], kn_g[l, 0], False)
        o_cmp, sel = _nsa_cmp(qn, k_cmp, v_cmp)
        o_win, o_sel = _nsa_attn(qn, qa, kw, vw, ks, vs, sel)
        x1, h2 = _out_projection(hm.reshape(n, M_WIDTH), o_cmp.reshape(n, A_WIDTH), o_sel.reshape(n, A_WIDTH),
                                 o_win.reshape(n, A_WIDTH), z2, x2, w_out[l], gate1, norm2_g[l], scale2, shift2, s_)
        eidx, gate = _peer_route(h2, peer_wq[l], peer_subkeys[l], s_)
        table = jnp.concatenate([peer_u[l], peer_v[l]], axis=1)[:, None, :]
        x = _peer_experts(eidx, gate, h2, x1, gate2, table, s_).reshape(b_, s_, d)
    return x
```

```python
import functools

import numpy as np
import jax
import jax.numpy as jnp
from jax import lax
from jax.experimental import pallas as pl
from jax.experimental.pallas import tpu as pltpu

F32 = jnp.float32
BF16 = jnp.bfloat16
HIGHEST = lax.Precision.HIGHEST

EPS = 1e-6
D_MODEL = 2048
M_HEADS = 4
M_HEAD_DIM = 256
M_WIDTH = M_HEADS * M_HEAD_DIM
M_CHUNK = 256
CONV_W = 4
A_GROUPS = 4
A_HPG = 4
A_HEADS = A_GROUPS * A_HPG
A_HEAD_DIM = 64
A_WIDTH = A_HEADS * A_HEAD_DIM
CMP_BLOCK = 32
CMP_STRIDE = 16
CMP_HIDDEN = 256
SEL_BLOCK = 64
SEL_TOP = 16
WINDOW = 512
Q_BLOCK = 128
FORCE_SCORE = 1e4
P_HEADS = 8
P_NKEYS = 128
P_TOPK = 16
P_QDIM = 256
P_HALF = 128
MV_OFF = 2 * M_WIDTH
MO_OFF = 3 * M_WIDTH
MIF_OFF = 4 * M_WIDTH
AQ_OFF = MIF_OFF + 2 * M_HEADS
AKV_OFF = AQ_OFF + A_WIDTH
AG_OFF = AKV_OFF + 6 * A_GROUPS * A_HEAD_DIM
IN_COLS = AG_OFF + 3 * A_HEADS
Z_M = 0
Z_AQ = 4096
Z_KV = 5120
Z_IF = 6656
Z_AG = 6784
Z_COLS = 6912
LANES = 128
NEG = -1e30
VMEM_LIMIT = 56 * 1024 * 1024


def _cparams(*sem):
    return pltpu.CompilerParams(dimension_semantics=sem, vmem_limit_bytes=VMEM_LIMIT)


def _gelu_tanh(x):
    return 0.5 * x * (1.0 + jnp.tanh(np.sqrt(2.0 / np.pi).astype(np.float32) * (x + 0.044715 * (x * x * x))))


def _mod_kernel(c_ref, w_ref, b_ref, o_ref):
    c = c_ref[...]
    s = c * jax.nn.sigmoid(c)
    o_ref[...] = jnp.dot(s, w_ref[...], preferred_element_type=F32, precision=HIGHEST) + b_ref[...]


def _modulation(c, w_mod, b_mod):
    b_, d = c.shape
    n = w_mod.shape[1]
    tn = 1024
    cp = jnp.pad(c, ((0, 8 - b_), (0, 0)))
    out = pl.pallas_call(
        _mod_kernel,
        grid=(n // tn,),
        in_specs=[pl.BlockSpec((8, d), lambda j: (0, 0)),
                  pl.BlockSpec((d, tn), lambda j: (0, j)),
                  pl.BlockSpec((1, tn), lambda j: (0, j))],
        out_specs=pl.BlockSpec((8, tn), lambda j: (0, j)),
        out_shape=jax.ShapeDtypeStruct((8, n), F32),
        compiler_params=_cparams("parallel"),
        name="modulation",
    )(cp, w_mod, b_mod.reshape(1, n))
    return out[:b_]


def _inproj_kernel(x_ref, g_ref, sc_ref, sh_ref, w_ref, o_ref, h_scr):
    @pl.when(pl.program_id(1) == 0)
    def _():
        x = x_ref[...]
        r = x * lax.rsqrt(jnp.mean(x * x, axis=-1, keepdims=True) + EPS)
        h = (r * g_ref[...]) * (1.0 + sc_ref[0]) + sh_ref[0]
        h_scr[...] = h.astype(BF16)

    o_ref[...] = jnp.dot(h_scr[...], w_ref[...], preferred_element_type=F32)


def _in_projection(x2, g1, scale1, shift1, w_pad, seq):
    n, d = x2.shape
    zc = w_pad.shape[1]
    tm = min(512, seq)
    tn = 768
    return pl.pallas_call(
        _inproj_kernel,
        grid=(n // tm, zc // tn),
        in_specs=[pl.BlockSpec((tm, d), lambda i, j: (i, 0)),
                  pl.BlockSpec((1, d), lambda i, j: (0, 0)),
                  pl.BlockSpec((1, 1, d), lambda i, j: ((i * tm) // seq, 0, 0)),
                  pl.BlockSpec((1, 1, d), lambda i, j: ((i * tm) // seq, 0, 0)),
                  pl.BlockSpec((d, tn), lambda i, j: (0, j))],
        out_specs=pl.BlockSpec((tm, tn), lambda i, j: (i, j)),
        out_shape=jax.ShapeDtypeStruct((n, zc), F32),
        scratch_shapes=[pltpu.VMEM((tm, d), BF16)],
        compiler_params=_cparams("parallel", "arbitrary"),
        name="in_projection",
    )(x2, g1.reshape(1, d), scale1, shift1, w_pad)


def _pad_in_weights(w_in):
    d = w_in.shape[0]
    z = lambda k: jnp.zeros((d, k), w_in.dtype)
    w = jnp.concatenate([
        w_in[:, :MIF_OFF],
        w_in[:, AQ_OFF:AKV_OFF],
        w_in[:, AKV_OFF:AG_OFF],
        w_in[:, MIF_OFF:AQ_OFF], z(LANES - 2 * M_HEADS),
        w_in[:, AG_OFF:IN_COLS], z(LANES - 3 * A_HEADS),
    ], axis=1)
    return w.astype(BF16)


def _mlstm_kernel(q_ref, k_ref, v_ref, o_ref, if_ref, cq_ref, ck_ref, bias_ref, g_ref, out_ref,
                  c_scr, n_scr, m_scr, qt_scr, kt_scr):
    L, d = q_ref.shape[1], q_ref.shape[2]
    h = pl.program_id(1)

    @pl.when(pl.program_id(2) == 0)
    def _():
        c_scr[...] = jnp.zeros_like(c_scr)
        n_scr[...] = jnp.zeros_like(n_scr)
        m_scr[...] = jnp.zeros_like(m_scr)
        qt_scr[...] = jnp.zeros_like(qt_scr)
        kt_scr[...] = jnp.zeros_like(kt_scr)

    def conv_silu(raw, tail_scr, w_ref):
        ext = jnp.concatenate([tail_scr[...], raw], axis=0)
        acc = jnp.zeros((L, d), F32)
        for j in range(CONV_W):
            off = 8 - (CONV_W - 1) + j
            acc = acc + w_ref[j:j + 1, :] * ext[off:off + L, :]
        tail_scr[...] = raw[L - 8:, :]
        return acc * jax.nn.sigmoid(acc)

    qc = conv_silu(q_ref[0], qt_scr, cq_ref)
    kc = conv_silu(k_ref[0], kt_scr, ck_ref) * (d ** -0.5)
    vc = v_ref[0]

    gt = if_ref[0] + bias_ref[...]
    lane = lax.broadcasted_iota(jnp.int32, (L, LANES), 1)
    logsig = jnp.minimum(gt, 0.0) - jnp.log(1.0 + jnp.exp(-jnp.abs(gt)))
    gt = jnp.where(lane < M_HEADS, gt, logsig)
    rows = lax.broadcasted_iota(jnp.int32, (L, L), 0)
    cols = lax.broadcasted_iota(jnp.int32, (L, L), 1)
    causal = rows >= cols
    tril = jnp.where(causal, 1.0, 0.0).astype(F32)
    csum = jnp.dot(tril, gt, preferred_element_type=F32, precision=HIGHEST)
    gt_t = gt.T
    csum_t = csum.T
    sub = lax.broadcasted_iota(jnp.int32, (LANES, L), 0)
    ig_col = jnp.sum(jnp.where(lane == h, gt, 0.0), axis=1, keepdims=True)
    b_col = jnp.sum(jnp.where(lane == h + M_HEADS, csum, 0.0), axis=1, keepdims=True)
    ig_row = jnp.sum(jnp.where(sub == h, gt_t, 0.0), axis=0, keepdims=True)
    b_row = jnp.sum(jnp.where(sub == h + M_HEADS, csum_t, 0.0), axis=0, keepdims=True)

    m_st = m_scr[0:1, 0:1]
    dmat = jnp.where(causal, b_col - b_row + ig_row, NEG)
    m_inter = b_col + m_st
    m_t = jnp.maximum(m_inter, jnp.max(dmat, axis=1, keepdims=True))
    qb = qc.astype(BF16)
    kb = kc.astype(BF16)
    vb = vc.astype(BF16)
    qk = lax.dot_general(qb, kb, (((1,), (1,)), ((), ())), preferred_element_type=F32)
    w = jnp.exp(dmat - m_t) * qk
    a_inter = jnp.exp(m_inter - m_t)
    c_st = c_scr[...]
    n_st = n_scr[...]
    num = (jnp.dot(w.astype(BF16), vb, preferred_element_type=F32)
           + a_inter * jnp.dot(qb, c_st.astype(BF16), preferred_element_type=F32))
    den = jnp.sum(w, axis=1, keepdims=True) + a_inter * jnp.sum(qc * n_st, axis=1, keepdims=True)
    hout = num / jnp.maximum(jnp.abs(den), jnp.exp(-m_t))

    b_last = b_col[L - 1:L, :]
    a_s = b_last - b_col + ig_col
    m_new = jnp.maximum(b_last + m_st, jnp.max(a_s, axis=0, keepdims=True))
    w_s = jnp.exp(a_s - m_new)
    decay = jnp.exp(b_last + m_st - m_new)
    kw = kc * w_s
    c_scr[...] = decay * c_st + lax.dot_general(kw.astype(BF16), vb, (((0,), (0,)), ((), ())),
                                                preferred_element_type=F32)
    n_scr[...] = decay * n_st + jnp.sum(kw, axis=0, keepdims=True)
    m_scr[...] = jnp.broadcast_to(m_new, m_scr.shape)

    r = hout * lax.rsqrt(jnp.mean(hout * hout, axis=-1, keepdims=True) + EPS)
    out_ref[0] = (r * g_ref[...]) * jax.nn.sigmoid(o_ref[0])


def _mlstm(z3, conv_qk, gate_bias, norm_g):
    b_, s_, _ = z3.shape
    L = min(M_CHUNK, s_)
    d = M_HEAD_DIM
    H = M_HEADS
    blk = lambda off: pl.BlockSpec((1, L, d), lambda b, h, c: (b, c, off + h))
    return pl.pallas_call(
        _mlstm_kernel,
        grid=(b_, H, s_ // L),
        in_specs=[blk(0), blk(H), blk(2 * H), blk(3 * H),
                  pl.BlockSpec((1, L, LANES), lambda b, h, c: (b, c, Z_IF // LANES)),
                  pl.BlockSpec((CONV_W, d), lambda b, h, c: (0, h)),
                  pl.BlockSpec((CONV_W, d), lambda b, h, c: (0, H + h)),
                  pl.BlockSpec((1, LANES), lambda b, h, c: (0, 0)),
                  pl.BlockSpec((1, d), lambda b, h, c: (0, h))],
        out_specs=pl.BlockSpec((1, L, d), lambda b, h, c: (b, c, h)),
        out_shape=jax.ShapeDtypeStruct((b_, s_, M_WIDTH), F32),
        scratch_shapes=[pltpu.VMEM((d, d), F32), pltpu.VMEM((1, d), F32), pltpu.VMEM((8, LANES), F32),
                        pltpu.VMEM((8, d), F32), pltpu.VMEM((8, d), F32)],
        compiler_params=_cparams("parallel", "parallel", "arbitrary"),
        name="mlstm",
    )(z3, z3, z3, z3, z3, conv_qk, conv_qk, gate_bias, norm_g.reshape(1, M_WIDTH))


SEL_TILE = 512
LOG2E = float(np.log2(np.e))


def _nsa_prep_kernel(aq_ref, kc_ref, vc_ref, ks_ref, vs_ref, kw_ref, vw_ref, qg_ref, kg_ref, slc_ref,
                     qn_o, qa_o, kc_o, vc_o, ks_o, vs_o, kw_o, vw_o):
    hd = A_HEAD_DIM
    ts = aq_ref.shape[1]

    def norm(xh, g):
        return (xh * lax.rsqrt(jnp.mean(xh * xh, axis=-1, keepdims=True) + EPS)) * g

    aq = aq_ref[0]
    qg = qg_ref[...]
    for h in range(A_HEADS):
        qh = norm(aq[:, h * hd:(h + 1) * hd], qg) * (hd ** -0.5)
        qn_o[0, h // A_HPG, h % A_HPG] = qh.astype(BF16)
        slope_cols = jnp.broadcast_to(slc_ref[h:h + 1, :], (ts, hd))
        qa_o[0, h // A_HPG, h % A_HPG] = jnp.concatenate([qh * LOG2E, slope_cols], axis=1).astype(BF16)
    pos = pl.program_id(1) * ts + lax.broadcasted_iota(jnp.int32, (ts, hd), 0)
    lane = lax.broadcasted_iota(jnp.int32, (ts, hd), 1)
    c = pos & (SEL_TILE - 1)
    kind = lambda r: (lane == r) | (lane == r + 3) | (lane == r + 6)
    pos_cols = jnp.where(kind(0), c & 255, jnp.where(kind(1), c & 256, jnp.where(kind(2), pos - c, 0))).astype(F32)
    kc, vc, ks, vs, kw, vw = kc_ref[0], vc_ref[0], ks_ref[0], vs_ref[0], kw_ref[0], vw_ref[0]
    for g in range(A_GROUPS):
        sl = slice(g * hd, (g + 1) * hd)
        kc_o[0, g] = kc[:, sl]
        vc_o[0, g] = vc[:, sl]
        ks_o[0, g] = jnp.concatenate([norm(ks[:, sl], kg_ref[1:2, :]), pos_cols], axis=1).astype(BF16)
        vs_o[0, g] = vs[:, sl].astype(BF16)
        kw_o[0, g] = norm(kw[:, sl], kg_ref[2:3, :]).astype(BF16)
        vw_o[0, g] = vw[:, sl].astype(BF16)


def _slope_cols_table():
    slopes = 2.0 ** (-8.0 * (np.arange(A_HEADS) + 1) / A_HEADS) * np.log2(np.e)
    tab = np.zeros((A_HEADS, A_HEAD_DIM), np.float32)
    rem = slopes.astype(np.float64)
    for i in range(3):
        part = rem.astype(np.float32).astype(BF16).astype(np.float64)
        tab[:, 3 * i:3 * i + 3] = part[:, None]
        rem = rem - part
    return jnp.asarray(tab)


def _nsa_prep(z3, qn_g, kn_g):
    b_, s_, _ = z3.shape
    ts = min(512, s_)
    G, hd = A_GROUPS, A_HEAD_DIM
    kvw = G * hd
    kv_in = lambda i: pl.BlockSpec((1, ts, kvw), lambda b, t: (b, t, Z_KV // kvw + i))
    kv_out = lambda w: pl.BlockSpec((1, G, ts, w), lambda b, t: (b, 0, t, 0))
    q_out = lambda w: pl.BlockSpec((1, G, A_HPG, ts, w), lambda b, t: (b, 0, 0, t, 0))
    kv_shape = lambda dt, w=hd: jax.ShapeDtypeStruct((b_, G, s_, w), dt)
    return pl.pallas_call(
        _nsa_prep_kernel,
        grid=(b_, s_ // ts),
        in_specs=[pl.BlockSpec((1, ts, A_WIDTH), lambda b, t: (b, t, Z_AQ // A_WIDTH))]
                 + [kv_in(i) for i in range(6)]
                 + [pl.BlockSpec((1, hd), lambda b, t: (0, 0)), pl.BlockSpec((3, hd), lambda b, t: (0, 0)),
                    pl.BlockSpec((A_HEADS, hd), lambda b, t: (0, 0))],
        out_specs=[q_out(hd), q_out(2 * hd), kv_out(hd), kv_out(hd), kv_out(2 * hd), kv_out(hd), kv_out(hd),
                   kv_out(hd)],
        out_shape=[jax.ShapeDtypeStruct((b_, G, A_HPG, s_, hd), BF16),
                   jax.ShapeDtypeStruct((b_, G, A_HPG, s_, 2 * hd), BF16),
                   kv_shape(F32), kv_shape(F32), kv_shape(BF16, 2 * hd), kv_shape(BF16), kv_shape(BF16),
                   kv_shape(BF16)],
        compiler_params=_cparams("parallel", "parallel"),
        name="nsa_prep",
    )(z3, z3, z3, z3, z3, z3, z3, qn_g.reshape(1, hd), kn_g, _slope_cols_table())


def _compress_kernel(r_ref, pos_ref, w1a_ref, w1b_ref, w2_ref, g_ref, o_ref, *, do_norm):
    r = r_ref[0, 0]
    nr = r.shape[0]
    u = jnp.dot((r + pos_ref[0:1, :]).astype(BF16), w1a_ref[...], preferred_element_type=F32)
    v = jnp.dot((r + pos_ref[1:2, :]).astype(BF16), w1b_ref[...], preferred_element_type=F32)
    pre = u + pltpu.roll(v, nr - 1, 0)
    out = jnp.dot(_gelu_tanh(pre).astype(BF16), w2_ref[...], preferred_element_type=F32)
    if do_norm:
        out = (out * lax.rsqrt(jnp.mean(out * out, axis=-1, keepdims=True) + EPS)) * g_ref[...]
    o_ref[0, 0] = out.astype(BF16)


def _compress(a, pos, w1, w2, g, do_norm):
    b_, G, s_, hd = a.shape
    nr = s_ // CMP_STRIDE
    half = CMP_STRIDE * hd
    r = a.reshape(b_, G, nr, half)
    w1b16 = w1.astype(BF16)
    return pl.pallas_call(
        functools.partial(_compress_kernel, do_norm=do_norm),
        grid=(b_, G),
        in_specs=[pl.BlockSpec((1, 1, nr, half), lambda b, g: (b, g, 0, 0)),
                  pl.BlockSpec((2, half), lambda b, g: (0, 0)),
                  pl.BlockSpec((half, CMP_HIDDEN), lambda b, g: (0, 0)),
                  pl.BlockSpec((half, CMP_HIDDEN), lambda b, g: (1, 0)),
                  pl.BlockSpec((CMP_HIDDEN, hd), lambda b, g: (0, 0)),
                  pl.BlockSpec((1, hd), lambda b, g: (0, 0))],
        out_specs=pl.BlockSpec((1, 1, nr, hd), lambda b, g: (b, g, 0, 0)),
        out_shape=jax.ShapeDtypeStruct((b_, G, nr, hd), BF16),
        compiler_params=_cparams("parallel", "parallel"),
        name="nsa_compress_norm" if do_norm else "nsa_compress",
    )(r, pos.reshape(2, half), w1b16, w1b16, w2.astype(BF16), g.reshape(1, hd))


def _nsa_cmp_kernel(q_ref, kc_ref, vc_ref, ov_ref, slope_ref, ocmp_ref, sel_ref, *, n_cmp, n_blk, n_sel):
    T, hd = q_ref.shape[3], q_ref.shape[4]
    R = A_HPG * T
    nc = kc_ref.shape[2]
    q0 = pl.program_id(2) * T
    q = q_ref[0, 0].reshape(R, hd)
    s = lax.dot_general(q, kc_ref[0, 0], (((1,), (1,)), ((), ())), preferred_element_type=F32)
    row = lax.broadcasted_iota(jnp.int32, (R, nc), 0)
    ci = lax.broadcasted_iota(jnp.int32, (R, nc), 1)
    t = q0 + (row & (T - 1))
    disti = t - (ci * CMP_STRIDE + (CMP_BLOCK - 1))
    valid = (disti >= 0) & (ci < n_cmp)
    slope = slope_ref[0][:, 0:1]
    s = jnp.where(valid, s - slope * disti.astype(F32), NEG)
    m = jnp.max(s, axis=1, keepdims=True)
    e = jnp.where(valid, jnp.exp(s - m), 0.0)
    p = e / jnp.maximum(jnp.sum(e, axis=1, keepdims=True), 1e-30)
    oc = jnp.dot(p.astype(BF16), vc_ref[0, 0], preferred_element_type=F32)
    for h in range(A_HPG):
        ocmp_ref[0, :, h * hd:(h + 1) * hd] = oc[h * T:(h + 1) * T, :]
    ps = p[0:T] + p[T:2 * T] + p[2 * T:3 * T] + p[3 * T:4 * T]
    hi = ps.astype(BF16)
    lo = (ps - hi.astype(F32)).astype(BF16)
    ov = ov_ref[...]
    imp = jnp.dot(hi, ov, preferred_element_type=F32) + jnp.dot(lo, ov, preferred_element_type=F32)
    imp = imp.T
    blk = lax.broadcasted_iota(jnp.int32, (LANES, T), 0)
    cur = (q0 + lax.broadcasted_iota(jnp.int32, (LANES, T), 1)) // SEL_BLOCK
    forced = (blk == 0) | (blk == cur) | (blk == cur - 1)
    imp = jnp.where(forced, FORCE_SCORE, jnp.where(blk <= cur, imp, -FORCE_SCORE))
    imp = jnp.where(blk < n_blk, imp, -3e38)
    sel = jnp.zeros((LANES, T), F32)
    for _ in range(n_sel):
        mx = jnp.max(imp, axis=0, keepdims=True)
        idx = jnp.min(jnp.where(imp == mx, blk, LANES), axis=0, keepdims=True)
        pick = blk == idx
        sel = jnp.where(pick, 1.0, sel)
        imp = jnp.where(pick, -3.4e38, imp)
    sel_ref[0, 0] = sel.T.astype(BF16)


def _slope_table():
    slopes = (2.0 ** (-8.0 * (np.arange(A_HEADS) + 1) / A_HEADS)).astype(np.float32).reshape(A_GROUPS, A_HPG)
    tab = np.repeat(slopes, Q_BLOCK, axis=1)
    return jnp.asarray(np.broadcast_to(tab[:, :, None], (A_GROUPS, A_HPG * Q_BLOCK, LANES)).copy())


def _overlap_table(s_, nc):
    n_cmp = (s_ - CMP_BLOCK) // CMP_STRIDE + 1
    n_blk = s_ // SEL_BLOCK
    starts = np.arange(nc)[:, None] * CMP_STRIDE
    blk_starts = np.arange(LANES)[None, :] * SEL_BLOCK
    ov = (starts < blk_starts + SEL_BLOCK) & (starts + CMP_BLOCK > blk_starts)
    ov &= (np.arange(nc)[:, None] < n_cmp) & (np.arange(LANES)[None, :] < n_blk)
    return jnp.asarray(ov.astype(np.float32)).astype(BF16)


def _nsa_cmp(qn, k_cmp, v_cmp):
    b_, G, hpg, s_, hd = qn.shape
    T = Q_BLOCK
    nc = k_cmp.shape[2]
    n_cmp = (s_ - CMP_BLOCK) // CMP_STRIDE + 1
    n_blk = s_ // SEL_BLOCK
    kern = functools.partial(_nsa_cmp_kernel, n_cmp=n_cmp, n_blk=n_blk, n_sel=min(SEL_TOP, n_blk))
    return pl.pallas_call(
        kern,
        grid=(b_, G, s_ // T),
        in_specs=[pl.BlockSpec((1, 1, hpg, T, hd), lambda b, g, i: (b, g, 0, i, 0)),
                  pl.BlockSpec((1, 1, nc, hd), lambda b, g, i: (b, g, 0, 0)),
                  pl.BlockSpec((1, 1, nc, hd), lambda b, g, i: (b, g, 0, 0)),
                  pl.BlockSpec((nc, LANES), lambda b, g, i: (0, 0)),
                  pl.BlockSpec((1, hpg * T, LANES), lambda b, g, i: (g, 0, 0))],
        out_specs=[pl.BlockSpec((1, T, hpg * hd), lambda b, g, i: (b, i, g)),
                   pl.BlockSpec((1, 1, T, LANES), lambda b, g, i: (b, g, i, 0))],
        out_shape=[jax.ShapeDtypeStruct((b_, s_, A_WIDTH), F32),
                   jax.ShapeDtypeStruct((b_, G, s_, LANES), BF16)],
        compiler_params=_cparams("parallel", "parallel", "parallel"),
        name="nsa_cmp_topk",
    )(qn, k_cmp, v_cmp, _overlap_table(s_, nc), _slope_table())


def _nsa_attn_kernel(q_ref, qa_ref, kw_ref, vw_ref, ks_ref, vs_ref, sel_ref, slope_ref, owin_ref, osel_ref,
                     m_scr, l_scr, acc_scr, *, wk, tk):
    T, hd = q_ref.shape[3], q_ref.shape[4]
    R = A_HPG * T
    qb = pl.program_id(2)
    q0 = qb * T
    q = q_ref[0, 0].reshape(R, hd)
    slope = slope_ref[0][:, 0:1]
    nt = (((1,), (1,)), ((), ()))

    start = pl.multiple_of(jnp.maximum(q0 + T - wk, 0), T)
    kw = kw_ref[0, 0, pl.ds(start, wk), :]
    vw = vw_ref[0, 0, pl.ds(start, wk), :]
    s = lax.dot_general(q, kw, nt, preferred_element_type=F32)
    t = q0 + (lax.broadcasted_iota(jnp.int32, (R, wk), 0) & (T - 1))
    disti = t - (start + lax.broadcasted_iota(jnp.int32, (R, wk), 1))
    mask = (disti >= 0) & (disti < WINDOW)
    s = jnp.where(mask, s - slope * disti.astype(F32), NEG)
    m = jnp.max(s, axis=1, keepdims=True)
    e = jnp.exp(s - m)
    ow = jnp.dot(e.astype(BF16), vw, preferred_element_type=F32) / jnp.sum(e, axis=1, keepdims=True)
    for h in range(A_HPG):
        owin_ref[0, :, h * hd:(h + 1) * hd] = ow[h * T:(h + 1) * T, :]

    qa = qa_ref[0, 0].reshape(R, 2 * hd)
    sel = sel_ref[0, 0]
    sel_bias = ((sel.astype(F32) - 1.0) * 1e30).astype(BF16)
    m_scr[...] = jnp.full_like(m_scr, NEG)
    l_scr[...] = jnp.zeros_like(l_scr)
    acc_scr[...] = jnp.zeros_like(acc_scr)
    bpt = tk // SEL_BLOCK
    lane = lax.broadcasted_iota(jnp.int32, (T, LANES), 1)
    j_diag = q0 // tk

    def tile(j, diagonal):
        kv0 = pl.multiple_of(j * tk, tk)
        bi = lax.broadcasted_iota(jnp.int32, (LANES, tk), 0)
        cc = lax.broadcasted_iota(jnp.int32, (LANES, tk), 1)
        expand = jnp.where(((kv0 + cc) // SEL_BLOCK) == bi, 1.0, 0.0).astype(BF16)
        bias = jnp.dot(sel_bias, expand, preferred_element_type=F32)
        k = ks_ref[0, 0, pl.ds(kv0, tk), :]
        v = vs_ref[0, 0, pl.ds(kv0, tk), :]
        sc = lax.dot_general(qa, k, nt, preferred_element_type=F32) + jnp.concatenate([bias] * A_HPG, axis=0)
        if diagonal:
            tt = q0 + (lax.broadcasted_iota(jnp.int32, (R, tk), 0) & (T - 1))
            sc = jnp.where(tt >= kv0 + lax.broadcasted_iota(jnp.int32, (R, tk), 1), sc, NEG)
        m_old = m_scr[...]
        m_new = jnp.maximum(m_old, jnp.max(sc, axis=1, keepdims=True))
        alpha = jnp.exp2(m_old - m_new)
        p = jnp.exp2(sc - m_new)
        l_scr[...] = alpha * l_scr[...] + jnp.sum(p, axis=1, keepdims=True)
        acc_scr[...] = alpha * acc_scr[...] + jnp.dot(p.astype(BF16), v, preferred_element_type=F32)
        m_scr[...] = m_new

    def body(j, carry):
        in_tile = (lane >= j * bpt) & (lane < (j + 1) * bpt)
        has = jnp.max(jnp.where(in_tile, sel.astype(F32), 0.0)) > 0.5

        @pl.when(jnp.logical_or(has, j == 0))
        def _():
            tile(j, False)

        return carry

    lax.fori_loop(0, j_diag, body, 0)
    tile(j_diag, True)
    os_ = acc_scr[...] / l_scr[...]
    for h in range(A_HPG):
        osel_ref[0, :, h * hd:(h + 1) * hd] = os_[h * T:(h + 1) * T, :]


def _nsa_attn(qn, qa, k_win, v_win, k_sel_aug, v_sel, sel):
    b_, G, hpg, s_, hd = qn.shape
    T = Q_BLOCK
    wk = min(WINDOW + T, s_)
    tk = SEL_TILE
    assert s_ % tk == 0
    full = lambda w: pl.BlockSpec((1, 1, s_, w), lambda b, g, i: (b, g, 0, 0))
    q_blk = lambda w: pl.BlockSpec((1, 1, hpg, T, w), lambda b, g, i: (b, g, 0, i, 0))
    out = pl.BlockSpec((1, T, hpg * hd), lambda b, g, i: (b, i, g))
    return pl.pallas_call(
        functools.partial(_nsa_attn_kernel, wk=wk, tk=tk),
        grid=(b_, G, s_ // T),
        in_specs=[q_blk(hd), q_blk(2 * hd),
                  full(hd), full(hd), full(2 * hd), full(hd),
                  pl.BlockSpec((1, 1, T, LANES), lambda b, g, i: (b, g, i, 0)),
                  pl.BlockSpec((1, hpg * T, LANES), lambda b, g, i: (g, 0, 0))],
        out_specs=[out, out],
        out_shape=[jax.ShapeDtypeStruct((b_, s_, A_WIDTH), F32)] * 2,
        scratch_shapes=[pltpu.VMEM((hpg * T, 1), F32), pltpu.VMEM((hpg * T, 1), F32),
                        pltpu.VMEM((hpg * T, hd), F32)],
        compiler_params=_cparams("parallel", "parallel", "arbitrary"),
        name="nsa_window_selected",
    )(qn, qa, k_win, v_win, k_sel_aug, v_sel, sel, _slope_table())


def _outproj_kernel(hm_ref, oc_ref, os_ref, ow_ref, ag_ref, ex_ref, x_ref, wo_ref, g1_ref, n2_ref, sc2_ref,
                    sh2_ref, x1_ref, h2_ref):
    sg = jax.nn.sigmoid(ag_ref[...])
    hi = sg.astype(BF16)
    lo = (sg - hi.astype(F32)).astype(BF16)
    ha = jnp.zeros(oc_ref.shape, F32)
    for br, o_ref in enumerate((oc_ref, os_ref, ow_ref)):
        ex = ex_ref[br]
        gexp = jnp.dot(hi, ex, preferred_element_type=F32) + jnp.dot(lo, ex, preferred_element_type=F32)
        ha = ha + gexp * o_ref[...]
    mw = hm_ref.shape[1]
    y = (jnp.dot(hm_ref[...].astype(BF16), wo_ref[0:mw, :], preferred_element_type=F32)
         + jnp.dot(ha.astype(BF16), wo_ref[mw:, :], preferred_element_type=F32))
    x1 = x_ref[...] + g1_ref[0] * y
    x1_ref[...] = x1
    r = x1 * lax.rsqrt(jnp.mean(x1 * x1, axis=-1, keepdims=True) + EPS)
    h2_ref[...] = (r * n2_ref[...]) * (1.0 + sc2_ref[0]) + sh2_ref[0]


def _gate_expand_table():
    ex = np.zeros((3, LANES, A_WIDTH), np.float32)
    for hd in range(A_HEADS):
        for br in range(3):
            ex[br, hd * 3 + br, hd * A_HEAD_DIM:(hd + 1) * A_HEAD_DIM] = 1.0
    return jnp.asarray(ex).astype(BF16)


def _out_projection(hm2, oc2, os2, ow2, z2, x2, w_out, gate1, norm2_g, scale2, shift2, seq):
    n, d = x2.shape
    tm = min(256, seq)
    row = lambda w: pl.BlockSpec((tm, w), lambda i: (i, 0))
    per_b = pl.BlockSpec((1, 1, d), lambda i: ((i * tm) // seq, 0, 0))
    return pl.pallas_call(
        _outproj_kernel,
        grid=(n // tm,),
        in_specs=[row(M_WIDTH), row(A_WIDTH), row(A_WIDTH), row(A_WIDTH),
                  pl.BlockSpec((tm, LANES), lambda i: (i, Z_AG // LANES)),
                  pl.BlockSpec((3, LANES, A_WIDTH), lambda i: (0, 0, 0)),
                  row(d),
                  pl.BlockSpec((M_WIDTH + A_WIDTH, d), lambda i: (0, 0)),
                  per_b,
                  pl.BlockSpec((1, d), lambda i: (0, 0)),
                  per_b, per_b],
        out_specs=[row(d), row(d)],
        out_shape=[jax.ShapeDtypeStruct((n, d), F32)] * 2,
        compiler_params=_cparams("parallel"),
        name="out_projection",
    )(hm2, oc2, os2, ow2, z2, _gate_expand_table(), x2, w_out.astype(BF16), gate1, norm2_g.reshape(1, d),
      scale2, shift2)


def _peer_route_kernel(h2_ref, wqt_ref, sk_ref, eidx_ref, gate_ref):
    tm = h2_ref.shape[0]
    nt = (((1,), (1,)), ((), ()))
    ninf = -3.4e38
    qt = lax.dot_general(wqt_ref[...], h2_ref[...].astype(BF16), nt, preferred_element_type=F32)
    k = P_TOPK
    rowk = lax.broadcasted_iota(jnp.int32, (k, tm), 0)

    def top_rows(sc, payload):
        nrow = sc.shape[0]
        rows = lax.broadcasted_iota(jnp.int32, (nrow, tm), 0)

        def body(r, carry):
            sc, vals, pay = carry
            mx = jnp.max(sc, axis=0, keepdims=True)
            idx = jnp.min(jnp.where(sc == mx, rows, nrow), axis=0, keepdims=True)
            pick = rows == idx
            got = idx if payload is None else jnp.sum(jnp.where(pick, payload, 0), axis=0, keepdims=True)
            vals = jnp.where(rowk == r, mx, vals)
            pay = jnp.where(rowk == r, got, pay)
            return jnp.where(pick, ninf, sc), vals, pay

        _, vals, pay = lax.fori_loop(0, k, body, (sc, jnp.zeros((k, tm), F32), jnp.zeros((k, tm), jnp.int32)))
        return vals, pay

    s1, i1 = top_rows(jnp.dot(sk_ref[0], qt[0:P_HALF], preferred_element_type=F32, precision=HIGHEST), None)
    s2, i2 = top_rows(jnp.dot(sk_ref[1], qt[P_HALF:], preferred_element_type=F32, precision=HIGHEST), None)
    row8 = lax.broadcasted_iota(jnp.int32, (8, tm), 0)
    cand = [s1[0:1] + s2]
    cidx = [i1[0:1] * P_NKEYS + i2]
    for a in range(1, 8):
        keep = row8 < (k // (a + 1))
        cand.append(jnp.where(keep, s1[a:a + 1] + s2[0:8], ninf))
        cidx.append(jnp.where(keep, i1[a:a + 1] * P_NKEYS + i2[0:8], 0))
    cand.append(s1[8:k] + s2[0:1])
    cidx.append(i1[8:k] * P_NKEYS + i2[0:1])
    top, e = top_rows(jnp.concatenate(cand, axis=0), jnp.concatenate(cidx, axis=0))
    ex = jnp.exp(top - top[0:1])
    eidx_ref[0] = e
    gate_ref[0] = ex / jnp.sum(ex, axis=0, keepdims=True)


def _peer_route(h2, wq, subkeys, seq):
    n, d = h2.shape
    tm = min(256, seq)
    wqt = wq.T.astype(BF16)
    out = pl.BlockSpec((1, P_TOPK, tm), lambda i, h: (h, 0, i))
    return pl.pallas_call(
        _peer_route_kernel,
        grid=(n // tm, P_HEADS),
        in_specs=[pl.BlockSpec((tm, d), lambda i, h: (i, 0)),
                  pl.BlockSpec((P_QDIM, d), lambda i, h: (h, 0)),
                  pl.BlockSpec((2, P_NKEYS, P_HALF), lambda i, h: (0, 0, 0))],
        out_specs=[out, out],
        out_shape=[jax.ShapeDtypeStruct((P_HEADS, P_TOPK, n), jnp.int32),
                   jax.ShapeDtypeStruct((P_HEADS, P_TOPK, n), F32)],
        compiler_params=_cparams("parallel", "arbitrary"),
        name="peer_route",
    )(h2, wqt, subkeys)


PEER_SEL = P_HEADS * P_TOPK
PEER_GRP = 4
PEER_SLOTS = 4
PEER_AHEAD = 2
PEER_TB = PEER_GRP * PEER_SLOTS


def _peer_expert_kernel(eidx_ref, enext_ref, gt_ref, h2_ref, x1_ref, g2_ref, tab_ref, o_ref, buf, sem):
    i = pl.program_id(0)
    d = h2_ref.shape[1]
    rows = PEER_GRP * PEER_SEL

    def issue(idx_ref, grp):
        for t in range(PEER_GRP):
            for j in range(PEER_SEL):
                e = idx_ref[grp * PEER_GRP + t, j]
                pltpu.make_async_copy(tab_ref.at[e], buf.at[grp, pl.ds(t * PEER_SEL + j, 1), :],
                                      sem.at[grp]).start()

    def wait(s):
        pltpu.make_async_copy(tab_ref.at[pl.ds(0, rows), 0, :], buf.at[s], sem.at[s]).wait()

    def mix(grp, t):
        tok = grp * PEER_GRP + t
        w = buf[grp, t * PEER_SEL:(t + 1) * PEER_SEL, :]
        a = jnp.sum(w[:, :d] * h2_ref[tok:tok + 1, :], axis=1, keepdims=True)
        wg = gt_ref[0][:, tok:tok + 1] * _gelu_tanh(a)
        y = jnp.sum(w[:, d:] * wg, axis=0, keepdims=True)
        o_ref[tok:tok + 1, :] = x1_ref[tok:tok + 1, :] + g2_ref[0] * y

    @pl.when(i == 0)
    def _():
        for grp in range(PEER_AHEAD):
            issue(eidx_ref, grp)

    for grp in range(PEER_SLOTS):
        nxt = grp + PEER_AHEAD
        if nxt < PEER_SLOTS:
            issue(eidx_ref, nxt)
        else:
            issue(enext_ref, nxt - PEER_SLOTS)
        wait(grp)
        for t in range(PEER_GRP):
            mix(grp, t)

    @pl.when(i == pl.num_programs(0) - 1)
    def _():
        for grp in range(PEER_AHEAD):
            wait(grp)


def _peer_experts(eidx_t, gate_t, h2, x1, gate2, table, seq):
    n, d = h2.shape
    tb = PEER_TB
    steps = n // tb
    eidx = eidx_t.reshape(PEER_SEL, n).T
    gate_t = gate_t.reshape(PEER_SEL, steps, tb).transpose(1, 0, 2)
    row = pl.BlockSpec((tb, d), lambda i: (i, 0))
    return pl.pallas_call(
        _peer_expert_kernel,
        grid=(steps,),
        in_specs=[pl.BlockSpec((tb, PEER_SEL), lambda i: (i, 0), memory_space=pltpu.SMEM),
                  pl.BlockSpec((tb, PEER_SEL), lambda i: (jnp.minimum(i + 1, steps - 1), 0),
                               memory_space=pltpu.SMEM),
                  pl.BlockSpec((1, PEER_SEL, tb), lambda i: (i, 0, 0)),
                  row, row,
                  pl.BlockSpec((1, 1, d), lambda i: ((i * tb) // seq, 0, 0)),
                  pl.BlockSpec(memory_space=pl.ANY)],
        out_specs=row,
        out_shape=jax.ShapeDtypeStruct((n, d), F32),
        scratch_shapes=[pltpu.VMEM((PEER_SLOTS, PEER_GRP * PEER_SEL, 2 * d), F32),
                        pltpu.SemaphoreType.DMA((PEER_SLOTS,))],
        compiler_params=_cparams("arbitrary"),
        name="peer_experts",
    )(eidx, eidx, gate_t, h2, x1, gate2, table)


def kernel(x, c, w_mod, b_mod, norm1_g, norm2_g, w_in, conv_qk, b_igate, b_fgate, mlstm_norm_g, qn_g, kn_g,
           cmp_pos_k, cmp_pos_v, cmp_k_w1, cmp_k_w2, cmp_v_w1, cmp_v_w2, w_out, peer_wq, peer_subkeys,
           peer_u, peer_v):
    b_, s_, d = x.shape
    n = b_ * s_
    for l in range(w_mod.shape[0]):
        mod = _modulation(c, w_mod[l], b_mod[l]).reshape(b_, 6, 1, d)
        shift1, scale1, gate1, shift2, scale2, gate2 = (mod[:, i] for i in range(6))
        x2 = x.reshape(n, d)
        z2 = _in_projection(x2, norm1_g[l], scale1, shift1, _pad_in_weights(w_in[l]), s_)
        z3 = z2.reshape(b_, s_, Z_COLS)
        gate_bias = jnp.concatenate([b_igate[l], b_fgate[l], jnp.zeros((LANES - 2 * M_HEADS,), F32)])
        hm = _mlstm(z3, conv_qk[l], gate_bias.reshape(1, LANES), mlstm_norm_g[l])
        qn, qa, kc, vc, ks, vs, kw, vw = _nsa_prep(z3, qn_g[l], kn_g[l])
        k_cmp = _compress(kc, cmp_pos_k[l], cmp_k_w1[l], cmp_k_w2[l], kn_g[l, 0], True)
        v_cmp = _compress(vc, cmp_pos_v[l], cmp_v_w1[l], cmp_v_w2[l], kn_g[l, 0], False)
        o_cmp, sel = _nsa_cmp(qn, k_cmp, v_cmp)
        o_win, o_sel = _nsa_attn(qn, qa, kw, vw, ks, vs, sel)
        x1, h2 = _out_projection(hm.reshape(n, M_WIDTH), o_cmp.reshape(n, A_WIDTH), o_sel.reshape(n, A_WIDTH),
                                 o_win.reshape(n, A_WIDTH), z2, x2, w_out[l], gate1, norm2_g[l], scale2, shift2, s_)
        eidx, gate = _peer_route(h2, peer_wq[l], peer_subkeys[l], s_)
        table = jnp.concatenate([peer_u[l][:, None, :], peer_v[l][:, None, :]], axis=2)
        x = _peer_experts(eidx, gate, h2, x1, gate2, table, s_).reshape(b_, s_, d)
    return x
```

```python
import functools

import numpy as np
import jax
import jax.numpy as jnp
from jax import lax
from jax.experimental import pallas as pl
from jax.experimental.pallas import tpu as pltpu

F32 = jnp.float32
BF16 = jnp.bfloat16
HIGHEST = lax.Precision.HIGHEST

EPS = 1e-6
D_MODEL = 2048
M_HEADS = 4
M_HEAD_DIM = 256
M_WIDTH = M_HEADS * M_HEAD_DIM
M_CHUNK = 256
CONV_W = 4
A_GROUPS = 4
A_HPG = 4
A_HEADS = A_GROUPS * A_HPG
A_HEAD_DIM = 64
A_WIDTH = A_HEADS * A_HEAD_DIM
CMP_BLOCK = 32
CMP_STRIDE = 16
CMP_HIDDEN = 256
SEL_BLOCK = 64
SEL_TOP = 16
WINDOW = 512
Q_BLOCK = 128
FORCE_SCORE = 1e4
P_HEADS = 8
P_NKEYS = 128
P_TOPK = 16
P_QDIM = 256
P_HALF = 128
MV_OFF = 2 * M_WIDTH
MO_OFF = 3 * M_WIDTH
MIF_OFF = 4 * M_WIDTH
AQ_OFF = MIF_OFF + 2 * M_HEADS
AKV_OFF = AQ_OFF + A_WIDTH
AG_OFF = AKV_OFF + 6 * A_GROUPS * A_HEAD_DIM
IN_COLS = AG_OFF + 3 * A_HEADS
Z_M = 0
Z_AQ = 4096
Z_KV = 5120
Z_IF = 6656
Z_AG = 6784
Z_COLS = 6912
LANES = 128
NEG = -1e30
VMEM_LIMIT = 56 * 1024 * 1024


def _cparams(*sem):
    return pltpu.CompilerParams(dimension_semantics=sem, vmem_limit_bytes=VMEM_LIMIT)


def _gelu_tanh(x):
    return 0.5 * x * (1.0 + jnp.tanh(np.sqrt(2.0 / np.pi).astype(np.float32) * (x + 0.044715 * (x * x * x))))


def _mod_kernel(c_ref, w_ref, b_ref, o_ref):
    c = c_ref[...]
    s = c * jax.nn.sigmoid(c)
    o_ref[...] = jnp.dot(s, w_ref[...], preferred_element_type=F32, precision=HIGHEST) + b_ref[...]


def _modulation(c, w_mod, b_mod):
    b_, d = c.shape
    n = w_mod.shape[1]
    tn = 1024
    cp = jnp.pad(c, ((0, 8 - b_), (0, 0)))
    out = pl.pallas_call(
        _mod_kernel,
        grid=(n // tn,),
        in_specs=[pl.BlockSpec((8, d), lambda j: (0, 0)),
                  pl.BlockSpec((d, tn), lambda j: (0, j)),
                  pl.BlockSpec((1, tn), lambda j: (0, j))],
        out_specs=pl.BlockSpec((8, tn), lambda j: (0, j)),
        out_shape=jax.ShapeDtypeStruct((8, n), F32),
        compiler_params=_cparams("parallel"),
        name="modulation",
    )(cp, w_mod, b_mod.reshape(1, n))
    return out[:b_]


def _inproj_kernel(x_ref, g_ref, sc_ref, sh_ref, w_ref, o_ref, h_scr):
    @pl.when(pl.program_id(1) == 0)
    def _():
        x = x_ref[...]
        r = x * lax.rsqrt(jnp.mean(x * x, axis=-1, keepdims=True) + EPS)
        h = (r * g_ref[...]) * (1.0 + sc_ref[0]) + sh_ref[0]
        h_scr[...] = h.astype(BF16)

    o_ref[...] = jnp.dot(h_scr[...], w_ref[...], preferred_element_type=F32)


def _in_projection(x2, g1, scale1, shift1, w_pad, seq):
    n, d = x2.shape
    zc = w_pad.shape[1]
    tm = min(1024, seq)
    tn = 768
    return pl.pallas_call(
        _inproj_kernel,
        grid=(n // tm, zc // tn),
        in_specs=[pl.BlockSpec((tm, d), lambda i, j: (i, 0)),
                  pl.BlockSpec((1, d), lambda i, j: (0, 0)),
                  pl.BlockSpec((1, 1, d), lambda i, j: ((i * tm) // seq, 0, 0)),
                  pl.BlockSpec((1, 1, d), lambda i, j: ((i * tm) // seq, 0, 0)),
                  pl.BlockSpec((d, tn), lambda i, j: (0, j))],
        out_specs=pl.BlockSpec((tm, tn), lambda i, j: (i, j)),
        out_shape=jax.ShapeDtypeStruct((n, zc), F32),
        scratch_shapes=[pltpu.VMEM((tm, d), BF16)],
        compiler_params=_cparams("parallel", "arbitrary"),
        name="in_projection",
    )(x2, g1.reshape(1, d), scale1, shift1, w_pad)


def _pad_in_weights(w_in):
    d = w_in.shape[0]
    z = lambda k: jnp.zeros((d, k), w_in.dtype)
    w = jnp.concatenate([
        w_in[:, :MIF_OFF],
        w_in[:, AQ_OFF:AKV_OFF],
        w_in[:, AKV_OFF:AG_OFF],
        w_in[:, MIF_OFF:AQ_OFF], z(LANES - 2 * M_HEADS),
        w_in[:, AG_OFF:IN_COLS], z(LANES - 3 * A_HEADS),
    ], axis=1)
    return w.astype(BF16)


def _mlstm_kernel(q_ref, k_ref, v_ref, o_ref, if_ref, cq_ref, ck_ref, bias_ref, g_ref, out_ref,
                  c_scr, n_scr, m_scr, qt_scr, kt_scr):
    L, d = q_ref.shape[1], q_ref.shape[2]
    h = pl.program_id(1)

    @pl.when(pl.program_id(2) == 0)
    def _():
        c_scr[...] = jnp.zeros_like(c_scr)
        n_scr[...] = jnp.zeros_like(n_scr)
        m_scr[...] = jnp.zeros_like(m_scr)
        qt_scr[...] = jnp.zeros_like(qt_scr)
        kt_scr[...] = jnp.zeros_like(kt_scr)

    def conv_silu(raw, tail_scr, w_ref):
        ext = jnp.concatenate([tail_scr[...], raw], axis=0)
        acc = jnp.zeros((L, d), F32)
        for j in range(CONV_W):
            off = 8 - (CONV_W - 1) + j
            acc = acc + w_ref[j:j + 1, :] * ext[off:off + L, :]
        tail_scr[...] = raw[L - 8:, :]
        return acc * jax.nn.sigmoid(acc)

    qc = conv_silu(q_ref[0], qt_scr, cq_ref)
    kc = conv_silu(k_ref[0], kt_scr, ck_ref) * (d ** -0.5)
    vc = v_ref[0]

    gt = if_ref[0] + bias_ref[...]
    lane = lax.broadcasted_iota(jnp.int32, (L, LANES), 1)
    logsig = jnp.minimum(gt, 0.0) - jnp.log(1.0 + jnp.exp(-jnp.abs(gt)))
    gt = jnp.where(lane < M_HEADS, gt, logsig)
    rows = lax.broadcasted_iota(jnp.int32, (L, L), 0)
    cols = lax.broadcasted_iota(jnp.int32, (L, L), 1)
    causal = rows >= cols
    tril = jnp.where(causal, 1.0, 0.0).astype(F32)
    csum = jnp.dot(tril, gt, preferred_element_type=F32, precision=HIGHEST)
    gt_t = gt.T
    csum_t = csum.T
    sub = lax.broadcasted_iota(jnp.int32, (LANES, L), 0)
    ig_col = jnp.sum(jnp.where(lane == h, gt, 0.0), axis=1, keepdims=True)
    b_col = jnp.sum(jnp.where(lane == h + M_HEADS, csum, 0.0), axis=1, keepdims=True)
    ig_row = jnp.sum(jnp.where(sub == h, gt_t, 0.0), axis=0, keepdims=True)
    b_row = jnp.sum(jnp.where(sub == h + M_HEADS, csum_t, 0.0), axis=0, keepdims=True)

    m_st = m_scr[0:1, 0:1]
    dmat = jnp.where(causal, b_col - b_row + ig_row, NEG)
    m_inter = b_col + m_st
    m_t = jnp.maximum(m_inter, jnp.max(dmat, axis=1, keepdims=True))
    qb = qc.astype(BF16)
    kb = kc.astype(BF16)
    vb = vc.astype(BF16)
    qk = lax.dot_general(qb, kb, (((1,), (1,)), ((), ())), preferred_element_type=F32)
    w = jnp.exp(dmat - m_t) * qk
    a_inter = jnp.exp(m_inter - m_t)
    c_st = c_scr[...]
    n_st = n_scr[...]
    num = (jnp.dot(w.astype(BF16), vb, preferred_element_type=F32)
           + a_inter * jnp.dot(qb, c_st.astype(BF16), preferred_element_type=F32))
    den = jnp.sum(w, axis=1, keepdims=True) + a_inter * jnp.sum(qc * n_st, axis=1, keepdims=True)
    hout = num / jnp.maximum(jnp.abs(den), jnp.exp(-m_t))

    b_last = b_col[L - 1:L, :]
    a_s = b_last - b_col + ig_col
    m_new = jnp.maximum(b_last + m_st, jnp.max(a_s, axis=0, keepdims=True))
    w_s = jnp.exp(a_s - m_new)
    decay = jnp.exp(b_last + m_st - m_new)
    kw = kc * w_s
    c_scr[...] = decay * c_st + lax.dot_general(kw.astype(BF16), vb, (((0,), (0,)), ((), ())),
                                                preferred_element_type=F32)
    n_scr[...] = decay * n_st + jnp.sum(kw, axis=0, keepdims=True)
    m_scr[...] = jnp.broadcast_to(m_new, m_scr.shape)

    r = hout * lax.rsqrt(jnp.mean(hout * hout, axis=-1, keepdims=True) + EPS)
    out_ref[0] = (r * g_ref[...]) * jax.nn.sigmoid(o_ref[0])


def _mlstm(z3, conv_qk, gate_bias, norm_g):
    b_, s_, _ = z3.shape
    L = min(M_CHUNK, s_)
    d = M_HEAD_DIM
    H = M_HEADS
    blk = lambda off: pl.BlockSpec((1, L, d), lambda b, h, c: (b, c, off + h))
    return pl.pallas_call(
        _mlstm_kernel,
        grid=(b_, H, s_ // L),
        in_specs=[blk(0), blk(H), blk(2 * H), blk(3 * H),
                  pl.BlockSpec((1, L, LANES), lambda b, h, c: (b, c, Z_IF // LANES)),
                  pl.BlockSpec((CONV_W, d), lambda b, h, c: (0, h)),
                  pl.BlockSpec((CONV_W, d), lambda b, h, c: (0, H + h)),
                  pl.BlockSpec((1, LANES), lambda b, h, c: (0, 0)),
                  pl.BlockSpec((1, d), lambda b, h, c: (0, h))],
        out_specs=pl.BlockSpec((1, L, d), lambda b, h, c: (b, c, h)),
        out_shape=jax.ShapeDtypeStruct((b_, s_, M_WIDTH), F32),
        scratch_shapes=[pltpu.VMEM((d, d), F32), pltpu.VMEM((1, d), F32), pltpu.VMEM((8, LANES), F32),
                        pltpu.VMEM((8, d), F32), pltpu.VMEM((8, d), F32)],
        compiler_params=_cparams("parallel", "parallel", "arbitrary"),
        name="mlstm",
    )(z3, z3, z3, z3, z3, conv_qk, conv_qk, gate_bias, norm_g.reshape(1, M_WIDTH))


SEL_TILE = 512
LOG2E = float(np.log2(np.e))


def _nsa_prep_kernel(aq_ref, kc_ref, vc_ref, ks_ref, vs_ref, kw_ref, vw_ref, qg_ref, kg_ref, slc_ref,
                     qn_o, qa_o, kc_o, vc_o, ks_o, vs_o, kw_o, vw_o):
    hd = A_HEAD_DIM
    ts = aq_ref.shape[1]

    def norm(xh, g):
        return (xh * lax.rsqrt(jnp.mean(xh * xh, axis=-1, keepdims=True) + EPS)) * g

    aq = aq_ref[0]
    qg = qg_ref[...]
    for h in range(A_HEADS):
        qh = norm(aq[:, h * hd:(h + 1) * hd], qg) * (hd ** -0.5)
        qn_o[0, h // A_HPG, h % A_HPG] = qh.astype(BF16)
        slope_cols = jnp.broadcast_to(slc_ref[h:h + 1, :], (ts, hd))
        qa_o[0, h // A_HPG, h % A_HPG] = jnp.concatenate([qh * LOG2E, slope_cols], axis=1).astype(BF16)
    pos = pl.program_id(1) * ts + lax.broadcasted_iota(jnp.int32, (ts, hd), 0)
    lane = lax.broadcasted_iota(jnp.int32, (ts, hd), 1)
    c = pos & (SEL_TILE - 1)
    kind = lambda r: (lane == r) | (lane == r + 3) | (lane == r + 6)
    pos_cols = jnp.where(kind(0), c & 255, jnp.where(kind(1), c & 256, jnp.where(kind(2), pos - c, 0))).astype(F32)
    kc, vc, ks, vs, kw, vw = kc_ref[0], vc_ref[0], ks_ref[0], vs_ref[0], kw_ref[0], vw_ref[0]
    for g in range(A_GROUPS):
        sl = slice(g * hd, (g + 1) * hd)
        kc_o[0, g] = kc[:, sl]
        vc_o[0, g] = vc[:, sl]
        ks_o[0, g] = jnp.concatenate([norm(ks[:, sl], kg_ref[1:2, :]), pos_cols], axis=1).astype(BF16)
        vs_o[0, g] = vs[:, sl].astype(BF16)
        kw_o[0, g] = norm(kw[:, sl], kg_ref[2:3, :]).astype(BF16)
        vw_o[0, g] = vw[:, sl].astype(BF16)


def _slope_cols_table():
    slopes = 2.0 ** (-8.0 * (np.arange(A_HEADS) + 1) / A_HEADS) * np.log2(np.e)
    tab = np.zeros((A_HEADS, A_HEAD_DIM), np.float32)
    rem = slopes.astype(np.float64)
    for i in range(3):
        part = rem.astype(np.float32).astype(BF16).astype(np.float64)
        tab[:, 3 * i:3 * i + 3] = part[:, None]
        rem = rem - part
    return jnp.asarray(tab)


def _nsa_prep(z3, qn_g, kn_g):
    b_, s_, _ = z3.shape
    ts = min(512, s_)
    G, hd = A_GROUPS, A_HEAD_DIM
    kvw = G * hd
    kv_in = lambda i: pl.BlockSpec((1, ts, kvw), lambda b, t: (b, t, Z_KV // kvw + i))
    kv_out = lambda w: pl.BlockSpec((1, G, ts, w), lambda b, t: (b, 0, t, 0))
    q_out = lambda w: pl.BlockSpec((1, G, A_HPG, ts, w), lambda b, t: (b, 0, 0, t, 0))
    kv_shape = lambda dt, w=hd: jax.ShapeDtypeStruct((b_, G, s_, w), dt)
    return pl.pallas_call(
        _nsa_prep_kernel,
        grid=(b_, s_ // ts),
        in_specs=[pl.BlockSpec((1, ts, A_WIDTH), lambda b, t: (b, t, Z_AQ // A_WIDTH))]
                 + [kv_in(i) for i in range(6)]
                 + [pl.BlockSpec((1, hd), lambda b, t: (0, 0)), pl.BlockSpec((3, hd), lambda b, t: (0, 0)),
                    pl.BlockSpec((A_HEADS, hd), lambda b, t: (0, 0))],
        out_specs=[q_out(hd), q_out(2 * hd), kv_out(hd), kv_out(hd), kv_out(2 * hd), kv_out(hd), kv_out(hd),
                   kv_out(hd)],
        out_shape=[jax.ShapeDtypeStruct((b_, G, A_HPG, s_, hd), BF16),
                   jax.ShapeDtypeStruct((b_, G, A_HPG, s_, 2 * hd), BF16),
                   kv_shape(F32), kv_shape(F32), kv_shape(BF16, 2 * hd), kv_shape(BF16), kv_shape(BF16),
                   kv_shape(BF16)],
        compiler_params=_cparams("parallel", "parallel"),
        name="nsa_prep",
    )(z3, z3, z3, z3, z3, z3, z3, qn_g.reshape(1, hd), kn_g, _slope_cols_table())


def _compress_kernel(r_ref, pos_ref, w1a_ref, w1b_ref, w2_ref, g_ref, o_ref, *, do_norm):
    r = r_ref[0, 0]
    nr = r.shape[0]
    u = jnp.dot((r + pos_ref[0:1, :]).astype(BF16), w1a_ref[...], preferred_element_type=F32)
    v = jnp.dot((r + pos_ref[1:2, :]).astype(BF16), w1b_ref[...], preferred_element_type=F32)
    pre = u + pltpu.roll(v, nr - 1, 0)
    out = jnp.dot(_gelu_tanh(pre).astype(BF16), w2_ref[...], preferred_element_type=F32)
    if do_norm:
        out = (out * lax.rsqrt(jnp.mean(out * out, axis=-1, keepdims=True) + EPS)) * g_ref[...]
    o_ref[0, 0] = out.astype(BF16)


def _compress(a, pos, w1, w2, g, do_norm):
    b_, G, s_, hd = a.shape
    nr = s_ // CMP_STRIDE
    half = CMP_STRIDE * hd
    r = a.reshape(b_, G, nr, half)
    w1b16 = w1.astype(BF16)
    return pl.pallas_call(
        functools.partial(_compress_kernel, do_norm=do_norm),
        grid=(b_, G),
        in_specs=[pl.BlockSpec((1, 1, nr, half), lambda b, g: (b, g, 0, 0)),
                  pl.BlockSpec((2, half), lambda b, g: (0, 0)),
                  pl.BlockSpec((half, CMP_HIDDEN), lambda b, g: (0, 0)),
                  pl.BlockSpec((half, CMP_HIDDEN), lambda b, g: (1, 0)),
                  pl.BlockSpec((CMP_HIDDEN, hd), lambda b, g: (0, 0)),
                  pl.BlockSpec((1, hd), lambda b, g: (0, 0))],
        out_specs=pl.BlockSpec((1, 1, nr, hd), lambda b, g: (b, g, 0, 0)),
        out_shape=jax.ShapeDtypeStruct((b_, G, nr, hd), BF16),
        compiler_params=_cparams("parallel", "parallel"),
        name="nsa_compress_norm" if do_norm else "nsa_compress",
    )(r, pos.reshape(2, half), w1b16, w1b16, w2.astype(BF16), g.reshape(1, hd))


def _nsa_cmp_kernel(q_ref, kc_ref, vc_ref, ov_ref, slope_ref, ocmp_ref, sel_ref, *, n_cmp, n_blk, n_sel):
    T, hd = q_ref.shape[3], q_ref.shape[4]
    R = A_HPG * T
    nc = kc_ref.shape[2]
    q0 = pl.program_id(2) * T
    q = q_ref[0, 0].reshape(R, hd)
    s = lax.dot_general(q, kc_ref[0, 0], (((1,), (1,)), ((), ())), preferred_element_type=F32)
    row = lax.broadcasted_iota(jnp.int32, (R, nc), 0)
    ci = lax.broadcasted_iota(jnp.int32, (R, nc), 1)
    t = q0 + (row & (T - 1))
    disti = t - (ci * CMP_STRIDE + (CMP_BLOCK - 1))
    valid = (disti >= 0) & (ci < n_cmp)
    slope = slope_ref[0][:, 0:1]
    s = jnp.where(valid, s - slope * disti.astype(F32), NEG)
    m = jnp.max(s, axis=1, keepdims=True)
    e = jnp.where(valid, jnp.exp(s - m), 0.0)
    p = e / jnp.maximum(jnp.sum(e, axis=1, keepdims=True), 1e-30)
    oc = jnp.dot(p.astype(BF16), vc_ref[0, 0], preferred_element_type=F32)
    for h in range(A_HPG):
        ocmp_ref[0, :, h * hd:(h + 1) * hd] = oc[h * T:(h + 1) * T, :]
    ps = p[0:T] + p[T:2 * T] + p[2 * T:3 * T] + p[3 * T:4 * T]
    hi = ps.astype(BF16)
    lo = (ps - hi.astype(F32)).astype(BF16)
    ov = ov_ref[...]
    imp = jnp.dot(hi, ov, preferred_element_type=F32) + jnp.dot(lo, ov, preferred_element_type=F32)
    imp = imp.T
    blk = lax.broadcasted_iota(jnp.int32, (LANES, T), 0)
    cur = (q0 + lax.broadcasted_iota(jnp.int32, (LANES, T), 1)) // SEL_BLOCK
    forced = (blk == 0) | (blk == cur) | (blk == cur - 1)
    imp = jnp.where(forced, FORCE_SCORE, jnp.where(blk <= cur, imp, -FORCE_SCORE))
    imp = jnp.where(blk < n_blk, imp, -3e38)
    sel = jnp.zeros((LANES, T), F32)
    for _ in range(n_sel):
        mx = jnp.max(imp, axis=0, keepdims=True)
        idx = jnp.min(jnp.where(imp == mx, blk, LANES), axis=0, keepdims=True)
        pick = blk == idx
        sel = jnp.where(pick, 1.0, sel)
        imp = jnp.where(pick, -3.4e38, imp)
    sel_ref[0, 0] = sel.T.astype(BF16)


def _slope_table():
    slopes = (2.0 ** (-8.0 * (np.arange(A_HEADS) + 1) / A_HEADS)).astype(np.float32).reshape(A_GROUPS, A_HPG)
    tab = np.repeat(slopes, Q_BLOCK, axis=1)
    return jnp.asarray(np.broadcast_to(tab[:, :, None], (A_GROUPS, A_HPG * Q_BLOCK, LANES)).copy())


def _overlap_table(s_, nc):
    n_cmp = (s_ - CMP_BLOCK) // CMP_STRIDE + 1
    n_blk = s_ // SEL_BLOCK
    starts = np.arange(nc)[:, None] * CMP_STRIDE
    blk_starts = np.arange(LANES)[None, :] * SEL_BLOCK
    ov = (starts < blk_starts + SEL_BLOCK) & (starts + CMP_BLOCK > blk_starts)
    ov &= (np.arange(nc)[:, None] < n_cmp) & (np.arange(LANES)[None, :] < n_blk)
    return jnp.asarray(ov.astype(np.float32)).astype(BF16)


def _nsa_cmp(qn, k_cmp, v_cmp):
    b_, G, hpg, s_, hd = qn.shape
    T = Q_BLOCK
    nc = k_cmp.shape[2]
    n_cmp = (s_ - CMP_BLOCK) // CMP_STRIDE + 1
    n_blk = s_ // SEL_BLOCK
    kern = functools.partial(_nsa_cmp_kernel, n_cmp=n_cmp, n_blk=n_blk, n_sel=min(SEL_TOP, n_blk))
    return pl.pallas_call(
        kern,
        grid=(b_, G, s_ // T),
        in_specs=[pl.BlockSpec((1, 1, hpg, T, hd), lambda b, g, i: (b, g, 0, i, 0)),
                  pl.BlockSpec((1, 1, nc, hd), lambda b, g, i: (b, g, 0, 0)),
                  pl.BlockSpec((1, 1, nc, hd), lambda b, g, i: (b, g, 0, 0)),
                  pl.BlockSpec((nc, LANES), lambda b, g, i: (0, 0)),
                  pl.BlockSpec((1, hpg * T, LANES), lambda b, g, i: (g, 0, 0))],
        out_specs=[pl.BlockSpec((1, T, hpg * hd), lambda b, g, i: (b, i, g)),
                   pl.BlockSpec((1, 1, T, LANES), lambda b, g, i: (b, g, i, 0))],
        out_shape=[jax.ShapeDtypeStruct((b_, s_, A_WIDTH), F32),
                   jax.ShapeDtypeStruct((b_, G, s_, LANES), BF16)],
        compiler_params=_cparams("parallel", "parallel", "parallel"),
        name="nsa_cmp_topk",
    )(qn, k_cmp, v_cmp, _overlap_table(s_, nc), _slope_table())


def _nsa_attn_kernel(q_ref, qa_ref, kw_ref, vw_ref, ks_ref, vs_ref, sel_ref, slope_ref, owin_ref, osel_ref,
                     m_scr, l_scr, acc_scr, *, wk, tk):
    T, hd = q_ref.shape[3], q_ref.shape[4]
    R = A_HPG * T
    qb = pl.program_id(2)
    q0 = qb * T
    q = q_ref[0, 0].reshape(R, hd)
    slope = slope_ref[0][:, 0:1]
    nt = (((1,), (1,)), ((), ()))

    start = pl.multiple_of(jnp.maximum(q0 + T - wk, 0), T)
    kw = kw_ref[0, 0, pl.ds(start, wk), :]
    vw = vw_ref[0, 0, pl.ds(start, wk), :]
    s = lax.dot_general(q, kw, nt, preferred_element_type=F32)
    t = q0 + (lax.broadcasted_iota(jnp.int32, (R, wk), 0) & (T - 1))
    disti = t - (start + lax.broadcasted_iota(jnp.int32, (R, wk), 1))
    mask = (disti >= 0) & (disti < WINDOW)
    s = jnp.where(mask, s - slope * disti.astype(F32), NEG)
    m = jnp.max(s, axis=1, keepdims=True)
    e = jnp.exp(s - m)
    ow = jnp.dot(e.astype(BF16), vw, preferred_element_type=F32) / jnp.sum(e, axis=1, keepdims=True)
    for h in range(A_HPG):
        owin_ref[0, :, h * hd:(h + 1) * hd] = ow[h * T:(h + 1) * T, :]

    qa = qa_ref[0, 0].reshape(R, 2 * hd)
    sel = sel_ref[0, 0]
    sel_bias = ((sel.astype(F32) - 1.0) * 1e30).astype(BF16)
    m_scr[...] = jnp.full_like(m_scr, NEG)
    l_scr[...] = jnp.zeros_like(l_scr)
    acc_scr[...] = jnp.zeros_like(acc_scr)
    bpt = tk // SEL_BLOCK
    lane = lax.broadcasted_iota(jnp.int32, (T, LANES), 1)
    j_diag = q0 // tk

    def tile(j, diagonal):
        kv0 = pl.multiple_of(j * tk, tk)
        bi = lax.broadcasted_iota(jnp.int32, (LANES, tk), 0)
        cc = lax.broadcasted_iota(jnp.int32, (LANES, tk), 1)
        expand = jnp.where(((kv0 + cc) // SEL_BLOCK) == bi, 1.0, 0.0).astype(BF16)
        bias = jnp.dot(sel_bias, expand, preferred_element_type=F32)
        k = ks_ref[0, 0, pl.ds(kv0, tk), :]
        v = vs_ref[0, 0, pl.ds(kv0, tk), :]
        sc = lax.dot_general(qa, k, nt, preferred_element_type=F32) + jnp.concatenate([bias] * A_HPG, axis=0)
        if diagonal:
            tt = q0 + (lax.broadcasted_iota(jnp.int32, (R, tk), 0) & (T - 1))
            sc = jnp.where(tt >= kv0 + lax.broadcasted_iota(jnp.int32, (R, tk), 1), sc, NEG)
        m_old = m_scr[...]
        m_new = jnp.maximum(m_old, jnp.max(sc, axis=1, keepdims=True))
        alpha = jnp.exp2(m_old - m_new)
        p = jnp.exp2(sc - m_new)
        l_scr[...] = alpha * l_scr[...] + jnp.sum(p, axis=1, keepdims=True)
        acc_scr[...] = alpha * acc_scr[...] + jnp.dot(p.astype(BF16), v, preferred_element_type=F32)
        m_scr[...] = m_new

    def body(j, carry):
        in_tile = (lane >= j * bpt) & (lane < (j + 1) * bpt)
        has = jnp.max(jnp.where(in_tile, sel.astype(F32), 0.0)) > 0.5

        @pl.when(has)
        def _():
            tile(j, False)

        return carry

    tile(j_diag, True)
    lax.fori_loop(0, j_diag, body, 0)
    os_ = acc_scr[...] / l_scr[...]
    for h in range(A_HPG):
        osel_ref[0, :, h * hd:(h + 1) * hd] = os_[h * T:(h + 1) * T, :]


def _nsa_attn(qn, qa, k_win, v_win, k_sel_aug, v_sel, sel):
    b_, G, hpg, s_, hd = qn.shape
    T = Q_BLOCK
    wk = min(WINDOW + T, s_)
    tk = SEL_TILE
    assert s_ % tk == 0
    full = lambda w: pl.BlockSpec((1, 1, s_, w), lambda b, g, i: (b, g, 0, 0))
    q_blk = lambda w: pl.BlockSpec((1, 1, hpg, T, w), lambda b, g, i: (b, g, 0, i, 0))
    out = pl.BlockSpec((1, T, hpg * hd), lambda b, g, i: (b, i, g))
    return pl.pallas_call(
        functools.partial(_nsa_attn_kernel, wk=wk, tk=tk),
        grid=(b_, G, s_ // T),
        in_specs=[q_blk(hd), q_blk(2 * hd),
                  full(hd), full(hd), full(2 * hd), full(hd),
                  pl.BlockSpec((1, 1, T, LANES), lambda b, g, i: (b, g, i, 0)),
                  pl.BlockSpec((1, hpg * T, LANES), lambda b, g, i: (g, 0, 0))],
        out_specs=[out, out],
        out_shape=[jax.ShapeDtypeStruct((b_, s_, A_WIDTH), F32)] * 2,
        scratch_shapes=[pltpu.VMEM((hpg * T, 1), F32), pltpu.VMEM((hpg * T, 1), F32),
                        pltpu.VMEM((hpg * T, hd), F32)],
        compiler_params=_cparams("parallel", "parallel", "arbitrary"),
        name="nsa_window_selected",
    )(qn, qa, k_win, v_win, k_sel_aug, v_sel, sel, _slope_table())


def _outproj_kernel(hm_ref, oc_ref, os_ref, ow_ref, ag_ref, ex_ref, x_ref, wo_ref, g1_ref, n2_ref, sc2_ref,
                    sh2_ref, x1_ref, h2_ref):
    sg = jax.nn.sigmoid(ag_ref[...])
    hi = sg.astype(BF16)
    lo = (sg - hi.astype(F32)).astype(BF16)
    ha = jnp.zeros(oc_ref.shape, F32)
    for br, o_ref in enumerate((oc_ref, os_ref, ow_ref)):
        ex = ex_ref[br]
        gexp = jnp.dot(hi, ex, preferred_element_type=F32) + jnp.dot(lo, ex, preferred_element_type=F32)
        ha = ha + gexp * o_ref[...]
    mw = hm_ref.shape[1]
    y = (jnp.dot(hm_ref[...].astype(BF16), wo_ref[0:mw, :], preferred_element_type=F32)
         + jnp.dot(ha.astype(BF16), wo_ref[mw:, :], preferred_element_type=F32))
    x1 = x_ref[...] + g1_ref[0] * y
    x1_ref[...] = x1
    r = x1 * lax.rsqrt(jnp.mean(x1 * x1, axis=-1, keepdims=True) + EPS)
    h2_ref[...] = (r * n2_ref[...]) * (1.0 + sc2_ref[0]) + sh2_ref[0]


def _gate_expand_table():
    ex = np.zeros((3, LANES, A_WIDTH), np.float32)
    for hd in range(A_HEADS):
        for br in range(3):
            ex[br, hd * 3 + br, hd * A_HEAD_DIM:(hd + 1) * A_HEAD_DIM] = 1.0
    return jnp.asarray(ex).astype(BF16)


def _out_projection(hm2, oc2, os2, ow2, z2, x2, w_out, gate1, norm2_g, scale2, shift2, seq):
    n, d = x2.shape
    tm = min(256, seq)
    row = lambda w: pl.BlockSpec((tm, w), lambda i: (i, 0))
    per_b = pl.BlockSpec((1, 1, d), lambda i: ((i * tm) // seq, 0, 0))
    return pl.pallas_call(
        _outproj_kernel,
        grid=(n // tm,),
        in_specs=[row(M_WIDTH), row(A_WIDTH), row(A_WIDTH), row(A_WIDTH),
                  pl.BlockSpec((tm, LANES), lambda i: (i, Z_AG // LANES)),
                  pl.BlockSpec((3, LANES, A_WIDTH), lambda i: (0, 0, 0)),
                  row(d),
                  pl.BlockSpec((M_WIDTH + A_WIDTH, d), lambda i: (0, 0)),
                  per_b,
                  pl.BlockSpec((1, d), lambda i: (0, 0)),
                  per_b, per_b],
        out_specs=[row(d), row(d)],
        out_shape=[jax.ShapeDtypeStruct((n, d), F32)] * 2,
        compiler_params=_cparams("parallel"),
        name="out_projection",
    )(hm2, oc2, os2, ow2, z2, _gate_expand_table(), x2, w_out.astype(BF16), gate1, norm2_g.reshape(1, d),
      scale2, shift2)


def _peer_route_kernel(h2_ref, wqt_ref, sk_ref, eidx_ref, gate_ref):
    tm = h2_ref.shape[0]
    nt = (((1,), (1,)), ((), ()))
    ninf = -3.4e38
    qt = lax.dot_general(wqt_ref[...], h2_ref[...].astype(BF16), nt, preferred_element_type=F32)
    k = P_TOPK
    rowk = lax.broadcasted_iota(jnp.int32, (k, tm), 0)

    def top_rows(sc, payload):
        nrow = sc.shape[0]
        rows = lax.broadcasted_iota(jnp.int32, (nrow, tm), 0)

        def body(r, carry):
            sc, vals, pay = carry
            mx = jnp.max(sc, axis=0, keepdims=True)
            idx = jnp.min(jnp.where(sc == mx, rows, nrow), axis=0, keepdims=True)
            pick = rows == idx
            got = idx if payload is None else jnp.sum(jnp.where(pick, payload, 0), axis=0, keepdims=True)
            vals = jnp.where(rowk == r, mx, vals)
            pay = jnp.where(rowk == r, got, pay)
            return jnp.where(pick, ninf, sc), vals, pay

        _, vals, pay = lax.fori_loop(0, k, body, (sc, jnp.zeros((k, tm), F32), jnp.zeros((k, tm), jnp.int32)))
        return vals, pay

    s1, i1 = top_rows(jnp.dot(sk_ref[0], qt[0:P_HALF], preferred_element_type=F32, precision=HIGHEST), None)
    s2, i2 = top_rows(jnp.dot(sk_ref[1], qt[P_HALF:], preferred_element_type=F32, precision=HIGHEST), None)
    row8 = lax.broadcasted_iota(jnp.int32, (8, tm), 0)
    cand = [s1[0:1] + s2]
    cidx = [i1[0:1] * P_NKEYS + i2]
    for a in range(1, 8):
        keep = row8 < (k // (a + 1))
        cand.append(jnp.where(keep, s1[a:a + 1] + s2[0:8], ninf))
        cidx.append(jnp.where(keep, i1[a:a + 1] * P_NKEYS + i2[0:8], 0))
    cand.append(s1[8:k] + s2[0:1])
    cidx.append(i1[8:k] * P_NKEYS + i2[0:1])
    top, e = top_rows(jnp.concatenate(cand, axis=0), jnp.concatenate(cidx, axis=0))
    ex = jnp.exp(top - top[0:1])
    eidx_ref[0] = e
    gate_ref[0] = ex / jnp.sum(ex, axis=0, keepdims=True)


def _peer_route(h2, wq, subkeys, seq):
    n, d = h2.shape
    tm = min(256, seq)
    wqt = wq.T.astype(BF16)
    out = pl.BlockSpec((1, P_TOPK, tm), lambda i, h: (h, 0, i))
    return pl.pallas_call(
        _peer_route_kernel,
        grid=(n // tm, P_HEADS),
        in_specs=[pl.BlockSpec((tm, d), lambda i, h: (i, 0)),
                  pl.BlockSpec((P_QDIM, d), lambda i, h: (h, 0)),
                  pl.BlockSpec((2, P_NKEYS, P_HALF), lambda i, h: (0, 0, 0))],
        out_specs=[out, out],
        out_shape=[jax.ShapeDtypeStruct((P_HEADS, P_TOPK, n), jnp.int32),
                   jax.ShapeDtypeStruct((P_HEADS, P_TOPK, n), F32)],
        compiler_params=_cparams("parallel", "arbitrary"),
        name="peer_route",
    )(h2, wqt, subkeys)


PEER_SEL = P_HEADS * P_TOPK
PEER_GRP = 4
PEER_SLOTS = 4
PEER_AHEAD = 2
PEER_TB = PEER_GRP * PEER_SLOTS


def _peer_expert_kernel(eidx_ref, enext_ref, gt_ref, h2_ref, x1_ref, g2_ref, tab_ref, o_ref, buf, sem):
    i = pl.program_id(0)
    d = h2_ref.shape[1]
    rows = PEER_GRP * PEER_SEL

    def issue(idx_ref, grp):
        for t in range(PEER_GRP):
            for j in range(PEER_SEL):
                e = idx_ref[grp * PEER_GRP + t, j]
                pltpu.make_async_copy(tab_ref.at[e], buf.at[grp, pl.ds(t * PEER_SEL + j, 1), :],
                                      sem.at[grp]).start()

    def wait(s):
        pltpu.make_async_copy(tab_ref.at[pl.ds(0, rows), 0, :], buf.at[s], sem.at[s]).wait()

    def mix(grp, t):
        tok = grp * PEER_GRP + t
        w = buf[grp, t * PEER_SEL:(t + 1) * PEER_SEL, :]
        a = jnp.sum(w[:, :d] * h2_ref[tok:tok + 1, :], axis=1, keepdims=True)
        wg = gt_ref[0][:, tok:tok + 1] * _gelu_tanh(a)
        y = jnp.sum(w[:, d:] * wg, axis=0, keepdims=True)
        o_ref[tok:tok + 1, :] = x1_ref[tok:tok + 1, :] + g2_ref[0] * y

    @pl.when(i == 0)
    def _():
        for grp in range(PEER_AHEAD):
            issue(eidx_ref, grp)

    for grp in range(PEER_SLOTS):
        nxt = grp + PEER_AHEAD
        if nxt < PEER_SLOTS:
            issue(eidx_ref, nxt)
        else:
            issue(enext_ref, nxt - PEER_SLOTS)
        wait(grp)
        for t in range(PEER_GRP):
            mix(grp, t)

    @pl.when(i == pl.num_programs(0) - 1)
    def _():
        for grp in range(PEER_AHEAD):
            wait(grp)


def _peer_experts(eidx_t, gate_t, h2, x1, gate2, table, seq):
    n, d = h2.shape
    tb = PEER_TB
    steps = n // tb
    eidx = eidx_t.reshape(PEER_SEL, n).T
    gate_t = gate_t.reshape(PEER_SEL, steps, tb).transpose(1, 0, 2)
    row = pl.BlockSpec((tb, d), lambda i: (i, 0))
    return pl.pallas_call(
        _peer_expert_kernel,
        grid=(steps,),
        in_specs=[pl.BlockSpec((tb, PEER_SEL), lambda i: (i, 0), memory_space=pltpu.SMEM),
                  pl.BlockSpec((tb, PEER_SEL), lambda i: (jnp.minimum(i + 1, steps - 1), 0),
                               memory_space=pltpu.SMEM),
                  pl.BlockSpec((1, PEER_SEL, tb), lambda i: (i, 0, 0)),
                  row, row,
                  pl.BlockSpec((1, 1, d), lambda i: ((i * tb) // seq, 0, 0)),
                  pl.BlockSpec(memory_space=pl.ANY)],
        out_specs=row,
        out_shape=jax.ShapeDtypeStruct((n, d), F32),
        scratch_shapes=[pltpu.VMEM((PEER_SLOTS, PEER_GRP * PEER_SEL, 2 * d), F32),
                        pltpu.SemaphoreType.DMA((PEER_SLOTS,))],
        compiler_params=_cparams("arbitrary"),
        name="peer_experts",
    )(eidx, eidx, gate_t, h2, x1, gate2, table)


def kernel(x, c, w_mod, b_mod, norm1_g, norm2_g, w_in, conv_qk, b_igate, b_fgate, mlstm_norm_g, qn_g, kn_g,
           cmp_pos_k, cmp_pos_v, cmp_k_w1, cmp_k_w2, cmp_v_w1, cmp_v_w2, w_out, peer_wq, peer_subkeys,
           peer_u, peer_v):
    b_, s_, d = x.shape
    n = b_ * s_
    for l in range(w_mod.shape[0]):
        mod = _modulation(c, w_mod[l], b_mod[l]).reshape(b_, 6, 1, d)
        shift1, scale1, gate1, shift2, scale2, gate2 = (mod[:, i] for i in range(6))
        x2 = x.reshape(n, d)
        z2 = _in_projection(x2, norm1_g[l], scale1, shift1, _pad_in_weights(w_in[l]), s_)
        z3 = z2.reshape(b_, s_, Z_COLS)
        gate_bias = jnp.concatenate([b_igate[l], b_fgate[l], jnp.zeros((LANES - 2 * M_HEADS,), F32)])
        hm = _mlstm(z3, conv_qk[l], gate_bias.reshape(1, LANES), mlstm_norm_g[l])
        qn, qa, kc, vc, ks, vs, kw, vw = _nsa_prep(z3, qn_g[l], kn_g[l])
        k_cmp = _compress(kc, cmp_pos_k[l], cmp_k_w1[l], cmp_k_w2[l], kn_g[l, 0], True)
        v_cmp = _compress(vc, cmp_pos_v[l], cmp_v_w1[l], cmp_v_w2[l], kn_g[l, 0], False)
        o_cmp, sel = _nsa_cmp(qn, k_cmp, v_cmp)
        o_win, o_sel = _nsa_attn(qn, qa, kw, vw, ks, vs, sel)
        x1, h2 = _out_projection(hm.reshape(n, M_WIDTH), o_cmp.reshape(n, A_WIDTH), o_sel.reshape(n, A_WIDTH),
                                 o_win.reshape(n, A_WIDTH), z2, x2, w_out[l], gate1, norm2_g[l], scale2, shift2, s_)
        eidx, gate = _peer_route(h2, peer_wq[l], peer_subkeys[l], s_)
        table = jnp.concatenate([peer_u[l], peer_v[l]], axis=1)[:, None, :]
        x = _peer_experts(eidx, gate, h2, x1, gate2, table, s_).reshape(b_, s_, d)
    return x
```

```python
import functools

import numpy as np
import jax
import jax.numpy as jnp
from jax import lax
from jax.experimental import pallas as pl
from jax.experimental.pallas import tpu as pltpu

F32 = jnp.float32
BF16 = jnp.bfloat16
HIGHEST = lax.Precision.HIGHEST

EPS = 1e-6
D_MODEL = 2048
M_HEADS = 4
M_HEAD_DIM = 256
M_WIDTH = M_HEADS * M_HEAD_DIM
M_CHUNK = 256
CONV_W = 4
A_GROUPS = 4
A_HPG = 4
A_HEADS = A_GROUPS * A_HPG
A_HEAD_DIM = 64
A_WIDTH = A_HEADS * A_HEAD_DIM
CMP_BLOCK = 32
CMP_STRIDE = 16
CMP_HIDDEN = 256
SEL_BLOCK = 64
SEL_TOP = 16
WINDOW = 512
Q_BLOCK = 128
FORCE_SCORE = 1e4
P_HEADS = 8
P_NKEYS = 128
P_TOPK = 16
P_QDIM = 256
P_HALF = 128
MV_OFF = 2 * M_WIDTH
MO_OFF = 3 * M_WIDTH
MIF_OFF = 4 * M_WIDTH
AQ_OFF = MIF_OFF + 2 * M_HEADS
AKV_OFF = AQ_OFF + A_WIDTH
AG_OFF = AKV_OFF + 6 * A_GROUPS * A_HEAD_DIM
IN_COLS = AG_OFF + 3 * A_HEADS
Z_M = 0
Z_AQ = 4096
Z_KV = 5120
Z_IF = 6656
Z_AG = 6784
Z_COLS = 6912
LANES = 128
NEG = -1e30
VMEM_LIMIT = 56 * 1024 * 1024


def _cparams(*sem):
    return pltpu.CompilerParams(dimension_semantics=sem, vmem_limit_bytes=VMEM_LIMIT)


def _gelu_tanh(x):
    return 0.5 * x * (1.0 + jnp.tanh(np.sqrt(2.0 / np.pi).astype(np.float32) * (x + 0.044715 * (x * x * x))))


def _mod_kernel(c_ref, w_ref, b_ref, o_ref):
    c = c_ref[...]
    s = c * jax.nn.sigmoid(c)
    o_ref[...] = jnp.dot(s, w_ref[...], preferred_element_type=F32, precision=HIGHEST) + b_ref[...]


def _modulation(c, w_mod, b_mod):
    b_, d = c.shape
    n = w_mod.shape[1]
    tn = 1024
    cp = jnp.pad(c, ((0, 8 - b_), (0, 0)))
    out = pl.pallas_call(
        _mod_kernel,
        grid=(n // tn,),
        in_specs=[pl.BlockSpec((8, d), lambda j: (0, 0)),
                  pl.BlockSpec((d, tn), lambda j: (0, j)),
                  pl.BlockSpec((1, tn), lambda j: (0, j))],
        out_specs=pl.BlockSpec((8, tn), lambda j: (0, j)),
        out_shape=jax.ShapeDtypeStruct((8, n), F32),
        compiler_params=_cparams("parallel"),
        name="modulation",
    )(cp, w_mod, b_mod.reshape(1, n))
    return out[:b_]


def _inproj_kernel(x_ref, g_ref, sc_ref, sh_ref, w_ref, o_ref, h_scr):
    @pl.when(pl.program_id(1) == 0)
    def _():
        x = x_ref[...]
        r = x * lax.rsqrt(jnp.mean(x * x, axis=-1, keepdims=True) + EPS)
        h = (r * g_ref[...]) * (1.0 + sc_ref[0]) + sh_ref[0]
        h_scr[...] = h.astype(BF16)

    o_ref[...] = jnp.dot(h_scr[...], w_ref[...], preferred_element_type=F32)


def _in_projection(x2, g1, scale1, shift1, w_pad, seq):
    n, d = x2.shape
    zc = w_pad.shape[1]
    tm = min(1024, seq)
    tn = 768
    return pl.pallas_call(
        _inproj_kernel,
        grid=(n // tm, zc // tn),
        in_specs=[pl.BlockSpec((tm, d), lambda i, j: (i, 0)),
                  pl.BlockSpec((1, d), lambda i, j: (0, 0)),
                  pl.BlockSpec((1, 1, d), lambda i, j: ((i * tm) // seq, 0, 0)),
                  pl.BlockSpec((1, 1, d), lambda i, j: ((i * tm) // seq, 0, 0)),
                  pl.BlockSpec((d, tn), lambda i, j: (0, j))],
        out_specs=pl.BlockSpec((tm, tn), lambda i, j: (i, j)),
        out_shape=jax.ShapeDtypeStruct((n, zc), F32),
        scratch_shapes=[pltpu.VMEM((tm, d), BF16)],
        compiler_params=_cparams("parallel", "arbitrary"),
        name="in_projection",
    )(x2, g1.reshape(1, d), scale1, shift1, w_pad)


def _pad_in_weights(w_in):
    d = w_in.shape[0]
    z = lambda k: jnp.zeros((d, k), w_in.dtype)
    w = jnp.concatenate([
        w_in[:, :MIF_OFF],
        w_in[:, AQ_OFF:AKV_OFF],
        w_in[:, AKV_OFF:AG_OFF],
        w_in[:, MIF_OFF:AQ_OFF], z(LANES - 2 * M_HEADS),
        w_in[:, AG_OFF:IN_COLS], z(LANES - 3 * A_HEADS),
    ], axis=1)
    return w.astype(BF16)


def _mlstm_kernel(q_ref, k_ref, v_ref, o_ref, if_ref, cq_ref, ck_ref, bias_ref, g_ref, out_ref,
                  c_scr, n_scr, m_scr, qt_scr, kt_scr):
    L, d = q_ref.shape[1], q_ref.shape[2]
    h = pl.program_id(1)

    @pl.when(pl.program_id(2) == 0)
    def _():
        c_scr[...] = jnp.zeros_like(c_scr)
        n_scr[...] = jnp.zeros_like(n_scr)
        m_scr[...] = jnp.zeros_like(m_scr)
        qt_scr[...] = jnp.zeros_like(qt_scr)
        kt_scr[...] = jnp.zeros_like(kt_scr)

    def conv_silu(raw, tail_scr, w_ref):
        ext = jnp.concatenate([tail_scr[...], raw], axis=0)
        acc = jnp.zeros((L, d), F32)
        for j in range(CONV_W):
            off = 8 - (CONV_W - 1) + j
            acc = acc + w_ref[j:j + 1, :] * ext[off:off + L, :]
        tail_scr[...] = raw[L - 8:, :]
        return acc * jax.nn.sigmoid(acc)

    qc = conv_silu(q_ref[0], qt_scr, cq_ref)
    kc = conv_silu(k_ref[0], kt_scr, ck_ref) * (d ** -0.5)
    vc = v_ref[0]

    gt = if_ref[0] + bias_ref[...]
    lane = lax.broadcasted_iota(jnp.int32, (L, LANES), 1)
    logsig = jnp.minimum(gt, 0.0) - jnp.log(1.0 + jnp.exp(-jnp.abs(gt)))
    gt = jnp.where(lane < M_HEADS, gt, logsig)
    rows = lax.broadcasted_iota(jnp.int32, (L, L), 0)
    cols = lax.broadcasted_iota(jnp.int32, (L, L), 1)
    causal = rows >= cols
    tril = jnp.where(causal, 1.0, 0.0).astype(F32)
    csum = jnp.dot(tril, gt, preferred_element_type=F32, precision=HIGHEST)
    gt_t = gt.T
    csum_t = csum.T
    sub = lax.broadcasted_iota(jnp.int32, (LANES, L), 0)
    ig_col = jnp.sum(jnp.where(lane == h, gt, 0.0), axis=1, keepdims=True)
    b_col = jnp.sum(jnp.where(lane == h + M_HEADS, csum, 0.0), axis=1, keepdims=True)
    ig_row = jnp.sum(jnp.where(sub == h, gt_t, 0.0), axis=0, keepdims=True)
    b_row = jnp.sum(jnp.where(sub == h + M_HEADS, csum_t, 0.0), axis=0, keepdims=True)

    m_st = m_scr[0:1, 0:1]
    dmat = jnp.where(causal, b_col - b_row + ig_row, NEG)
    m_inter = b_col + m_st
    m_t = jnp.maximum(m_inter, jnp.max(dmat, axis=1, keepdims=True))
    qb = qc.astype(BF16)
    kb = kc.astype(BF16)
    vb = vc.astype(BF16)
    qk = lax.dot_general(qb, kb, (((1,), (1,)), ((), ())), preferred_element_type=F32)
    w = jnp.exp(dmat - m_t) * qk
    a_inter = jnp.exp(m_inter - m_t)
    c_st = c_scr[...]
    n_st = n_scr[...]
    num = (jnp.dot(w.astype(BF16), vb, preferred_element_type=F32)
           + a_inter * jnp.dot(qb, c_st.astype(BF16), preferred_element_type=F32))
    den = jnp.sum(w, axis=1, keepdims=True) + a_inter * jnp.sum(qc * n_st, axis=1, keepdims=True)
    hout = num / jnp.maximum(jnp.abs(den), jnp.exp(-m_t))

    b_last = b_col[L - 1:L, :]
    a_s = b_last - b_col + ig_col
    m_new = jnp.maximum(b_last + m_st, jnp.max(a_s, axis=0, keepdims=True))
    w_s = jnp.exp(a_s - m_new)
    decay = jnp.exp(b_last + m_st - m_new)
    kw = kc * w_s
    c_scr[...] = decay * c_st + lax.dot_general(kw.astype(BF16), vb, (((0,), (0,)), ((), ())),
                                                preferred_element_type=F32)
    n_scr[...] = decay * n_st + jnp.sum(kw, axis=0, keepdims=True)
    m_scr[...] = jnp.broadcast_to(m_new, m_scr.shape)

    r = hout * lax.rsqrt(jnp.mean(hout * hout, axis=-1, keepdims=True) + EPS)
    out_ref[0] = (r * g_ref[...]) * jax.nn.sigmoid(o_ref[0])


def _mlstm(z3, conv_qk, gate_bias, norm_g):
    b_, s_, _ = z3.shape
    L = min(M_CHUNK, s_)
    d = M_HEAD_DIM
    H = M_HEADS
    blk = lambda off: pl.BlockSpec((1, L, d), lambda b, h, c: (b, c, off + h))
    return pl.pallas_call(
        _mlstm_kernel,
        grid=(b_, H, s_ // L),
        in_specs=[blk(0), blk(H), blk(2 * H), blk(3 * H),
                  pl.BlockSpec((1, L, LANES), lambda b, h, c: (b, c, Z_IF // LANES)),
                  pl.BlockSpec((CONV_W, d), lambda b, h, c: (0, h)),
                  pl.BlockSpec((CONV_W, d), lambda b, h, c: (0, H + h)),
                  pl.BlockSpec((1, LANES), lambda b, h, c: (0, 0)),
                  pl.BlockSpec((1, d), lambda b, h, c: (0, h))],
        out_specs=pl.BlockSpec((1, L, d), lambda b, h, c: (b, c, h)),
        out_shape=jax.ShapeDtypeStruct((b_, s_, M_WIDTH), F32),
        scratch_shapes=[pltpu.VMEM((d, d), F32), pltpu.VMEM((1, d), F32), pltpu.VMEM((8, LANES), F32),
                        pltpu.VMEM((8, d), F32), pltpu.VMEM((8, d), F32)],
        compiler_params=_cparams("parallel", "parallel", "arbitrary"),
        name="mlstm",
    )(z3, z3, z3, z3, z3, conv_qk, conv_qk, gate_bias, norm_g.reshape(1, M_WIDTH))


SEL_TILE = 512
LOG2E = float(np.log2(np.e))


def _nsa_prep_kernel(aq_ref, kc_ref, vc_ref, ks_ref, vs_ref, kw_ref, vw_ref, qg_ref, kg_ref, slc_ref,
                     qn_o, qa_o, kc_o, vc_o, ks_o, vs_o, kw_o, vw_o):
    hd = A_HEAD_DIM
    ts = aq_ref.shape[1]

    def norm(xh, g):
        return (xh * lax.rsqrt(jnp.mean(xh * xh, axis=-1, keepdims=True) + EPS)) * g

    aq = aq_ref[0]
    qg = qg_ref[...]
    for h in range(A_HEADS):
        qh = norm(aq[:, h * hd:(h + 1) * hd], qg) * (hd ** -0.5)
        qn_o[0, h // A_HPG, h % A_HPG] = qh.astype(BF16)
        slope_cols = jnp.broadcast_to(slc_ref[h:h + 1, :], (ts, hd))
        qa_o[0, h // A_HPG, h % A_HPG] = jnp.concatenate([qh * LOG2E, slope_cols], axis=1).astype(BF16)
    pos = pl.program_id(1) * ts + lax.broadcasted_iota(jnp.int32, (ts, hd), 0)
    lane = lax.broadcasted_iota(jnp.int32, (ts, hd), 1)
    c = pos & (SEL_TILE - 1)
    kind = lambda r: (lane == r) | (lane == r + 3) | (lane == r + 6)
    pos_cols = jnp.where(kind(0), c & 255, jnp.where(kind(1), c & 256, jnp.where(kind(2), pos - c, 0))).astype(F32)
    kc, vc, ks, vs, kw, vw = kc_ref[0], vc_ref[0], ks_ref[0], vs_ref[0], kw_ref[0], vw_ref[0]
    for g in range(A_GROUPS):
        sl = slice(g * hd, (g + 1) * hd)
        kc_o[0, g] = kc[:, sl]
        vc_o[0, g] = vc[:, sl]
        ks_o[0, g] = jnp.concatenate([norm(ks[:, sl], kg_ref[1:2, :]), pos_cols], axis=1).astype(BF16)
        vs_o[0, g] = vs[:, sl].astype(BF16)
        kw_o[0, g] = norm(kw[:, sl], kg_ref[2:3, :]).astype(BF16)
        vw_o[0, g] = vw[:, sl].astype(BF16)


def _slope_cols_table():
    slopes = 2.0 ** (-8.0 * (np.arange(A_HEADS) + 1) / A_HEADS) * np.log2(np.e)
    tab = np.zeros((A_HEADS, A_HEAD_DIM), np.float32)
    rem = slopes.astype(np.float64)
    for i in range(3):
        part = rem.astype(np.float32).astype(BF16).astype(np.float64)
        tab[:, 3 * i:3 * i + 3] = part[:, None]
        rem = rem - part
    return jnp.asarray(tab)


def _nsa_prep(z3, qn_g, kn_g):
    b_, s_, _ = z3.shape
    ts = min(512, s_)
    G, hd = A_GROUPS, A_HEAD_DIM
    kvw = G * hd
    kv_in = lambda i: pl.BlockSpec((1, ts, kvw), lambda b, t: (b, t, Z_KV // kvw + i))
    kv_out = lambda w: pl.BlockSpec((1, G, ts, w), lambda b, t: (b, 0, t, 0))
    q_out = lambda w: pl.BlockSpec((1, G, A_HPG, ts, w), lambda b, t: (b, 0, 0, t, 0))
    kv_shape = lambda dt, w=hd: jax.ShapeDtypeStruct((b_, G, s_, w), dt)
    return pl.pallas_call(
        _nsa_prep_kernel,
        grid=(b_, s_ // ts),
        in_specs=[pl.BlockSpec((1, ts, A_WIDTH), lambda b, t: (b, t, Z_AQ // A_WIDTH))]
                 + [kv_in(i) for i in range(6)]
                 + [pl.BlockSpec((1, hd), lambda b, t: (0, 0)), pl.BlockSpec((3, hd), lambda b, t: (0, 0)),
                    pl.BlockSpec((A_HEADS, hd), lambda b, t: (0, 0))],
        out_specs=[q_out(hd), q_out(2 * hd), kv_out(hd), kv_out(hd), kv_out(2 * hd), kv_out(hd), kv_out(hd),
                   kv_out(hd)],
        out_shape=[jax.ShapeDtypeStruct((b_, G, A_HPG, s_, hd), BF16),
                   jax.ShapeDtypeStruct((b_, G, A_HPG, s_, 2 * hd), BF16),
                   kv_shape(F32), kv_shape(F32), kv_shape(BF16, 2 * hd), kv_shape(BF16), kv_shape(BF16),
                   kv_shape(BF16)],
        compiler_params=_cparams("parallel", "parallel"),
        name="nsa_prep",
    )(z3, z3, z3, z3, z3, z3, z3, qn_g.reshape(1, hd), kn_g, _slope_cols_table())


def _compress_kernel(r_ref, pos_ref, w1a_ref, w1b_ref, w2_ref, g_ref, o_ref, *, do_norm):
    r = r_ref[0, 0]
    nr = r.shape[0]
    u = jnp.dot((r + pos_ref[0:1, :]).astype(BF16), w1a_ref[...], preferred_element_type=F32)
    v = jnp.dot((r + pos_ref[1:2, :]).astype(BF16), w1b_ref[...], preferred_element_type=F32)
    pre = u + pltpu.roll(v, nr - 1, 0)
    out = jnp.dot(_gelu_tanh(pre).astype(BF16), w2_ref[...], preferred_element_type=F32)
    if do_norm:
        out = (out * lax.rsqrt(jnp.mean(out * out, axis=-1, keepdims=True) + EPS)) * g_ref[...]
    o_ref[0, 0] = out.astype(BF16)


def _compress(a, pos, w1, w2, g, do_norm):
    b_, G, s_, hd = a.shape
    nr = s_ // CMP_STRIDE
    half = CMP_STRIDE * hd
    r = a.reshape(b_, G, nr, half)
    w1b16 = w1.astype(BF16)
    return pl.pallas_call(
        functools.partial(_compress_kernel, do_norm=do_norm),
        grid=(b_, G),
        in_specs=[pl.BlockSpec((1, 1, nr, half), lambda b, g: (b, g, 0, 0)),
                  pl.BlockSpec((2, half), lambda b, g: (0, 0)),
                  pl.BlockSpec((half, CMP_HIDDEN), lambda b, g: (0, 0)),
                  pl.BlockSpec((half, CMP_HIDDEN), lambda b, g: (1, 0)),
                  pl.BlockSpec((CMP_HIDDEN, hd), lambda b, g: (0, 0)),
                  pl.BlockSpec((1, hd), lambda b, g: (0, 0))],
        out_specs=pl.BlockSpec((1, 1, nr, hd), lambda b, g: (b, g, 0, 0)),
        out_shape=jax.ShapeDtypeStruct((b_, G, nr, hd), BF16),
        compiler_params=_cparams("parallel", "parallel"),
        name="nsa_compress_norm" if do_norm else "nsa_compress",
    )(r, pos.reshape(2, half), w1b16, w1b16, w2.astype(BF16), g.reshape(1, hd))


def _nsa_cmp_kernel(q_ref, kc_ref, vc_ref, ov_ref, slope_ref, ocmp_ref, sel_ref, *, n_cmp, n_blk, n_sel):
    T, hd = q_ref.shape[3], q_ref.shape[4]
    R = A_HPG * T
    nc = kc_ref.shape[2]
    q0 = pl.program_id(2) * T
    q = q_ref[0, 0].reshape(R, hd)
    s = lax.dot_general(q, kc_ref[0, 0], (((1,), (1,)), ((), ())), preferred_element_type=F32)
    row = lax.broadcasted_iota(jnp.int32, (R, nc), 0)
    ci = lax.broadcasted_iota(jnp.int32, (R, nc), 1)
    t = q0 + (row & (T - 1))
    disti = t - (ci * CMP_STRIDE + (CMP_BLOCK - 1))
    valid = (disti >= 0) & (ci < n_cmp)
    slope = slope_ref[0][:, 0:1]
    s = jnp.where(valid, s - slope * disti.astype(F32), NEG)
    m = jnp.max(s, axis=1, keepdims=True)
    e = jnp.where(valid, jnp.exp(s - m), 0.0)
    p = e / jnp.maximum(jnp.sum(e, axis=1, keepdims=True), 1e-30)
    oc = jnp.dot(p.astype(BF16), vc_ref[0, 0], preferred_element_type=F32)
    for h in range(A_HPG):
        ocmp_ref[0, :, h * hd:(h + 1) * hd] = oc[h * T:(h + 1) * T, :]
    ps = p[0:T] + p[T:2 * T] + p[2 * T:3 * T] + p[3 * T:4 * T]
    hi = ps.astype(BF16)
    lo = (ps - hi.astype(F32)).astype(BF16)
    ov = ov_ref[...]
    imp = jnp.dot(hi, ov, preferred_element_type=F32) + jnp.dot(lo, ov, preferred_element_type=F32)
    imp = imp.T
    blk = lax.broadcasted_iota(jnp.int32, (LANES, T), 0)
    cur = (q0 + lax.broadcasted_iota(jnp.int32, (LANES, T), 1)) // SEL_BLOCK
    forced = (blk == 0) | (blk == cur) | (blk == cur - 1)
    imp = jnp.where(forced, FORCE_SCORE, jnp.where(blk <= cur, imp, -FORCE_SCORE))
    imp = jnp.where(blk < n_blk, imp, -3e38)
    sel = jnp.zeros((LANES, T), F32)
    for _ in range(n_sel):
        mx = jnp.max(imp, axis=0, keepdims=True)
        idx = jnp.min(jnp.where(imp == mx, blk, LANES), axis=0, keepdims=True)
        pick = blk == idx
        sel = jnp.where(pick, 1.0, sel)
        imp = jnp.where(pick, -3.4e38, imp)
    sel_ref[0, 0] = sel.T.astype(BF16)


def _slope_table():
    slopes = (2.0 ** (-8.0 * (np.arange(A_HEADS) + 1) / A_HEADS)).astype(np.float32).reshape(A_GROUPS, A_HPG)
    tab = np.repeat(slopes, Q_BLOCK, axis=1)
    return jnp.asarray(np.broadcast_to(tab[:, :, None], (A_GROUPS, A_HPG * Q_BLOCK, LANES)).copy())


def _overlap_table(s_, nc):
    n_cmp = (s_ - CMP_BLOCK) // CMP_STRIDE + 1
    n_blk = s_ // SEL_BLOCK
    starts = np.arange(nc)[:, None] * CMP_STRIDE
    blk_starts = np.arange(LANES)[None, :] * SEL_BLOCK
    ov = (starts < blk_starts + SEL_BLOCK) & (starts + CMP_BLOCK > blk_starts)
    ov &= (np.arange(nc)[:, None] < n_cmp) & (np.arange(LANES)[None, :] < n_blk)
    return jnp.asarray(ov.astype(np.float32)).astype(BF16)


def _nsa_cmp(qn, k_cmp, v_cmp):
    b_, G, hpg, s_, hd = qn.shape
    T = Q_BLOCK
    nc = k_cmp.shape[2]
    n_cmp = (s_ - CMP_BLOCK) // CMP_STRIDE + 1
    n_blk = s_ // SEL_BLOCK
    kern = functools.partial(_nsa_cmp_kernel, n_cmp=n_cmp, n_blk=n_blk, n_sel=min(SEL_TOP, n_blk))
    return pl.pallas_call(
        kern,
        grid=(b_, G, s_ // T),
        in_specs=[pl.BlockSpec((1, 1, hpg, T, hd), lambda b, g, i: (b, g, 0, i, 0)),
                  pl.BlockSpec((1, 1, nc, hd), lambda b, g, i: (b, g, 0, 0)),
                  pl.BlockSpec((1, 1, nc, hd), lambda b, g, i: (b, g, 0, 0)),
                  pl.BlockSpec((nc, LANES), lambda b, g, i: (0, 0)),
                  pl.BlockSpec((1, hpg * T, LANES), lambda b, g, i: (g, 0, 0))],
        out_specs=[pl.BlockSpec((1, T, hpg * hd), lambda b, g, i: (b, i, g)),
                   pl.BlockSpec((1, 1, T, LANES), lambda b, g, i: (b, g, i, 0))],
        out_shape=[jax.ShapeDtypeStruct((b_, s_, A_WIDTH), F32),
                   jax.ShapeDtypeStruct((b_, G, s_, LANES), BF16)],
        compiler_params=_cparams("parallel", "parallel", "parallel"),
        name="nsa_cmp_topk",
    )(qn, k_cmp, v_cmp, _overlap_table(s_, nc), _slope_table())


def _nsa_attn_kernel(q_ref, qa_ref, kw_ref, vw_ref, ks_ref, vs_ref, sel_ref, slope_ref, owin_ref, osel_ref,
                     m_scr, l_scr, acc_scr, *, wk, tk):
    T, hd = q_ref.shape[3], q_ref.shape[4]
    R = A_HPG * T
    qb = pl.program_id(2)
    q0 = qb * T
    q = q_ref[0, 0].reshape(R, hd)
    slope = slope_ref[0][:, 0:1]
    nt = (((1,), (1,)), ((), ()))

    start = pl.multiple_of(jnp.maximum(q0 + T - wk, 0), T)
    kw = kw_ref[0, 0, pl.ds(start, wk), :]
    vw = vw_ref[0, 0, pl.ds(start, wk), :]
    s = lax.dot_general(q, kw, nt, preferred_element_type=F32)
    t = q0 + (lax.broadcasted_iota(jnp.int32, (R, wk), 0) & (T - 1))
    disti = t - (start + lax.broadcasted_iota(jnp.int32, (R, wk), 1))
    mask = (disti >= 0) & (disti < WINDOW)
    s = jnp.where(mask, s - slope * disti.astype(F32), NEG)
    m = jnp.max(s, axis=1, keepdims=True)
    e = jnp.exp(s - m)
    ow = jnp.dot(e.astype(BF16), vw, preferred_element_type=F32) / jnp.sum(e, axis=1, keepdims=True)
    for h in range(A_HPG):
        owin_ref[0, :, h * hd:(h + 1) * hd] = ow[h * T:(h + 1) * T, :]

    qa = qa_ref[0, 0].reshape(R, 2 * hd)
    sel = sel_ref[0, 0]
    sel_bias = ((sel.astype(F32) - 1.0) * 1e30).astype(BF16)
    m_scr[...] = jnp.full_like(m_scr, NEG)
    l_scr[...] = jnp.zeros_like(l_scr)
    acc_scr[...] = jnp.zeros_like(acc_scr)
    bpt = tk // SEL_BLOCK
    lane = lax.broadcasted_iota(jnp.int32, (T, LANES), 1)
    j_diag = q0 // tk

    def tile(j, diagonal, penalty=None):
        kv0 = pl.multiple_of(j * tk, tk)
        bi = lax.broadcasted_iota(jnp.int32, (LANES, tk), 0)
        cc = lax.broadcasted_iota(jnp.int32, (LANES, tk), 1)
        expand = jnp.where(((kv0 + cc) // SEL_BLOCK) == bi, 1.0, 0.0).astype(BF16)
        bias = jnp.dot(sel_bias, expand, preferred_element_type=F32)
        if penalty is not None:
            bias = bias + penalty
        k = ks_ref[0, 0, pl.ds(kv0, tk), :]
        v = vs_ref[0, 0, pl.ds(kv0, tk), :]
        sc = lax.dot_general(qa, k, nt, preferred_element_type=F32) + jnp.concatenate([bias] * A_HPG, axis=0)
        if diagonal:
            tt = q0 + (lax.broadcasted_iota(jnp.int32, (R, tk), 0) & (T - 1))
            sc = jnp.where(tt >= kv0 + lax.broadcasted_iota(jnp.int32, (R, tk), 1), sc, NEG)
        m_old = m_scr[...]
        m_new = jnp.maximum(m_old, jnp.max(sc, axis=1, keepdims=True))
        alpha = jnp.exp2(m_old - m_new)
        p = jnp.exp2(sc - m_new)
        l_scr[...] = alpha * l_scr[...] + jnp.sum(p, axis=1, keepdims=True)
        acc_scr[...] = alpha * acc_scr[...] + jnp.dot(p.astype(BF16), v, preferred_element_type=F32)
        m_scr[...] = m_new

    def body(j, carry):
        in_tile = (lane >= j * bpt) & (lane < (j + 1) * bpt)
        has = jnp.max(jnp.where(in_tile, sel.astype(F32), 0.0)) > 0.5

        @pl.when(has)
        def _():
            tile(j, False)

        return carry

    tile(j_diag, True)
    tile(0, False, jnp.where(j_diag == 0, NEG, 0.0))
    lax.fori_loop(1, j_diag, body, 0)
    os_ = acc_scr[...] / l_scr[...]
    for h in range(A_HPG):
        osel_ref[0, :, h * hd:(h + 1) * hd] = os_[h * T:(h + 1) * T, :]


def _nsa_attn(qn, qa, k_win, v_win, k_sel_aug, v_sel, sel):
    b_, G, hpg, s_, hd = qn.shape
    T = Q_BLOCK
    wk = min(WINDOW + T, s_)
    tk = SEL_TILE
    assert s_ % tk == 0
    full = lambda w: pl.BlockSpec((1, 1, s_, w), lambda b, g, i: (b, g, 0, 0))
    q_blk = lambda w: pl.BlockSpec((1, 1, hpg, T, w), lambda b, g, i: (b, g, 0, i, 0))
    out = pl.BlockSpec((1, T, hpg * hd), lambda b, g, i: (b, i, g))
    return pl.pallas_call(
        functools.partial(_nsa_attn_kernel, wk=wk, tk=tk),
        grid=(b_, G, s_ // T),
        in_specs=[q_blk(hd), q_blk(2 * hd),
                  full(hd), full(hd), full(2 * hd), full(hd),
                  pl.BlockSpec((1, 1, T, LANES), lambda b, g, i: (b, g, i, 0)),
                  pl.BlockSpec((1, hpg * T, LANES), lambda b, g, i: (g, 0, 0))],
        out_specs=[out, out],
        out_shape=[jax.ShapeDtypeStruct((b_, s_, A_WIDTH), F32)] * 2,
        scratch_shapes=[pltpu.VMEM((hpg * T, 1), F32), pltpu.VMEM((hpg * T, 1), F32),
                        pltpu.VMEM((hpg * T, hd), F32)],
        compiler_params=_cparams("parallel", "parallel", "arbitrary"),
        name="nsa_window_selected",
    )(qn, qa, k_win, v_win, k_sel_aug, v_sel, sel, _slope_table())


def _outproj_kernel(hm_ref, oc_ref, os_ref, ow_ref, ag_ref, ex_ref, x_ref, wo_ref, g1_ref, n2_ref, sc2_ref,
                    sh2_ref, x1_ref, h2_ref):
    sg = jax.nn.sigmoid(ag_ref[...])
    hi = sg.astype(BF16)
    lo = (sg - hi.astype(F32)).astype(BF16)
    ha = jnp.zeros(oc_ref.shape, F32)
    for br, o_ref in enumerate((oc_ref, os_ref, ow_ref)):
        ex = ex_ref[br]
        gexp = jnp.dot(hi, ex, preferred_element_type=F32) + jnp.dot(lo, ex, preferred_element_type=F32)
        ha = ha + gexp * o_ref[...]
    mw = hm_ref.shape[1]
    y = (jnp.dot(hm_ref[...].astype(BF16), wo_ref[0:mw, :], preferred_element_type=F32)
         + jnp.dot(ha.astype(BF16), wo_ref[mw:, :], preferred_element_type=F32))
    x1 = x_ref[...] + g1_ref[0] * y
    x1_ref[...] = x1
    r = x1 * lax.rsqrt(jnp.mean(x1 * x1, axis=-1, keepdims=True) + EPS)
    h2_ref[...] = (r * n2_ref[...]) * (1.0 + sc2_ref[0]) + sh2_ref[0]


def _gate_expand_table():
    ex = np.zeros((3, LANES, A_WIDTH), np.float32)
    for hd in range(A_HEADS):
        for br in range(3):
            ex[br, hd * 3 + br, hd * A_HEAD_DIM:(hd + 1) * A_HEAD_DIM] = 1.0
    return jnp.asarray(ex).astype(BF16)


def _out_projection(hm2, oc2, os2, ow2, z2, x2, w_out, gate1, norm2_g, scale2, shift2, seq):
    n, d = x2.shape
    tm = min(256, seq)
    row = lambda w: pl.BlockSpec((tm, w), lambda i: (i, 0))
    per_b = pl.BlockSpec((1, 1, d), lambda i: ((i * tm) // seq, 0, 0))
    return pl.pallas_call(
        _outproj_kernel,
        grid=(n // tm,),
        in_specs=[row(M_WIDTH), row(A_WIDTH), row(A_WIDTH), row(A_WIDTH),
                  pl.BlockSpec((tm, LANES), lambda i: (i, Z_AG // LANES)),
                  pl.BlockSpec((3, LANES, A_WIDTH), lambda i: (0, 0, 0)),
                  row(d),
                  pl.BlockSpec((M_WIDTH + A_WIDTH, d), lambda i: (0, 0)),
                  per_b,
                  pl.BlockSpec((1, d), lambda i: (0, 0)),
                  per_b, per_b],
        out_specs=[row(d), row(d)],
        out_shape=[jax.ShapeDtypeStruct((n, d), F32)] * 2,
        compiler_params=_cparams("parallel"),
        name="out_projection",
    )(hm2, oc2, os2, ow2, z2, _gate_expand_table(), x2, w_out.astype(BF16), gate1, norm2_g.reshape(1, d),
      scale2, shift2)


def _peer_route_kernel(h2_ref, wqt_ref, sk_ref, eidx_ref, gate_ref):
    tm = h2_ref.shape[0]
    nt = (((1,), (1,)), ((), ()))
    ninf = -3.4e38
    qt = lax.dot_general(wqt_ref[...], h2_ref[...].astype(BF16), nt, preferred_element_type=F32)
    k = P_TOPK
    rowk = lax.broadcasted_iota(jnp.int32, (k, tm), 0)

    def top_rows(sc, payload):
        nrow = sc.shape[0]
        rows = lax.broadcasted_iota(jnp.int32, (nrow, tm), 0)

        def body(r, carry):
            sc, vals, pay = carry
            mx = jnp.max(sc, axis=0, keepdims=True)
            idx = jnp.min(jnp.where(sc == mx, rows, nrow), axis=0, keepdims=True)
            pick = rows == idx
            got = idx if payload is None else jnp.sum(jnp.where(pick, payload, 0), axis=0, keepdims=True)
            vals = jnp.where(rowk == r, mx, vals)
            pay = jnp.where(rowk == r, got, pay)
            return jnp.where(pick, ninf, sc), vals, pay

        _, vals, pay = lax.fori_loop(0, k, body, (sc, jnp.zeros((k, tm), F32), jnp.zeros((k, tm), jnp.int32)))
        return vals, pay

    s1, i1 = top_rows(jnp.dot(sk_ref[0], qt[0:P_HALF], preferred_element_type=F32, precision=HIGHEST), None)
    s2, i2 = top_rows(jnp.dot(sk_ref[1], qt[P_HALF:], preferred_element_type=F32, precision=HIGHEST), None)
    row8 = lax.broadcasted_iota(jnp.int32, (8, tm), 0)
    cand = [s1[0:1] + s2]
    cidx = [i1[0:1] * P_NKEYS + i2]
    for a in range(1, 8):
        keep = row8 < (k // (a + 1))
        cand.append(jnp.where(keep, s1[a:a + 1] + s2[0:8], ninf))
        cidx.append(jnp.where(keep, i1[a:a + 1] * P_NKEYS + i2[0:8], 0))
    cand.append(s1[8:k] + s2[0:1])
    cidx.append(i1[8:k] * P_NKEYS + i2[0:1])
    top, e = top_rows(jnp.concatenate(cand, axis=0), jnp.concatenate(cidx, axis=0))
    ex = jnp.exp(top - top[0:1])
    eidx_ref[0] = e
    gate_ref[0] = ex / jnp.sum(ex, axis=0, keepdims=True)


def _peer_route(h2, wq, subkeys, seq):
    n, d = h2.shape
    tm = min(512, seq)
    wqt = wq.T.astype(BF16)
    out = pl.BlockSpec((1, P_TOPK, tm), lambda i, h: (h, 0, i))
    return pl.pallas_call(
        _peer_route_kernel,
        grid=(n // tm, P_HEADS),
        in_specs=[pl.BlockSpec((tm, d), lambda i, h: (i, 0)),
                  pl.BlockSpec((P_QDIM, d), lambda i, h: (h, 0)),
                  pl.BlockSpec((2, P_NKEYS, P_HALF), lambda i, h: (0, 0, 0))],
        out_specs=[out, out],
        out_shape=[jax.ShapeDtypeStruct((P_HEADS, P_TOPK, n), jnp.int32),
                   jax.ShapeDtypeStruct((P_HEADS, P_TOPK, n), F32)],
        compiler_params=_cparams("parallel", "arbitrary"),
        name="peer_route",
    )(h2, wqt, subkeys)


PEER_SEL = P_HEADS * P_TOPK
PEER_GRP = 4
PEER_SLOTS = 4
PEER_AHEAD = 2
PEER_TB = PEER_GRP * PEER_SLOTS


def _peer_expert_kernel(eidx_ref, enext_ref, gt_ref, h2_ref, x1_ref, g2_ref, tab_ref, o_ref, buf, sem):
    i = pl.program_id(0)
    d = h2_ref.shape[1]
    rows = PEER_GRP * PEER_SEL

    def issue(idx_ref, grp):
        for t in range(PEER_GRP):
            for j in range(PEER_SEL):
                e = idx_ref[grp * PEER_GRP + t, j]
                pltpu.make_async_copy(tab_ref.at[e], buf.at[grp, pl.ds(t * PEER_SEL + j, 1), :],
                                      sem.at[grp]).start()

    def wait(s):
        pltpu.make_async_copy(tab_ref.at[pl.ds(0, rows), 0, :], buf.at[s], sem.at[s]).wait()

    def mix(grp, t):
        tok = grp * PEER_GRP + t
        w = buf[grp, t * PEER_SEL:(t + 1) * PEER_SEL, :]
        a = jnp.sum(w[:, :d] * h2_ref[tok:tok + 1, :], axis=1, keepdims=True)
        wg = gt_ref[0][:, tok:tok + 1] * _gelu_tanh(a)
        y = jnp.sum(w[:, d:] * wg, axis=0, keepdims=True)
        o_ref[tok:tok + 1, :] = x1_ref[tok:tok + 1, :] + g2_ref[0] * y

    @pl.when(i == 0)
    def _():
        for grp in range(PEER_AHEAD):
            issue(eidx_ref, grp)

    for grp in range(PEER_SLOTS):
        nxt = grp + PEER_AHEAD
        if nxt < PEER_SLOTS:
            issue(eidx_ref, nxt)
        else:
            issue(enext_ref, nxt - PEER_SLOTS)
        wait(grp)
        for t in range(PEER_GRP):
            mix(grp, t)

    @pl.when(i == pl.num_programs(0) - 1)
    def _():
        for grp in range(PEER_AHEAD):
            wait(grp)


def _peer_experts(eidx_t, gate_t, h2, x1, gate2, table, seq):
    n, d = h2.shape
    tb = PEER_TB
    steps = n // tb
    eidx = eidx_t.reshape(PEER_SEL, n).T
    gate_t = gate_t.reshape(PEER_SEL, steps, tb).transpose(1, 0, 2)
    row = pl.BlockSpec((tb, d), lambda i: (i, 0))
    return pl.pallas_call(
        _peer_expert_kernel,
        grid=(steps,),
        in_specs=[pl.BlockSpec((tb, PEER_SEL), lambda i: (i, 0), memory_space=pltpu.SMEM),
                  pl.BlockSpec((tb, PEER_SEL), lambda i: (jnp.minimum(i + 1, steps - 1), 0),
                               memory_space=pltpu.SMEM),
                  pl.BlockSpec((1, PEER_SEL, tb), lambda i: (i, 0, 0)),
                  row, row,
                  pl.BlockSpec((1, 1, d), lambda i: ((i * tb) // seq, 0, 0)),
                  pl.BlockSpec(memory_space=pl.ANY)],
        out_specs=row,
        out_shape=jax.ShapeDtypeStruct((n, d), F32),
        scratch_shapes=[pltpu.VMEM((PEER_SLOTS, PEER_GRP * PEER_SEL, 2 * d), F32),
                        pltpu.SemaphoreType.DMA((PEER_SLOTS,))],
        compiler_params=_cparams("arbitrary"),
        name="peer_experts",
    )(eidx, eidx, gate_t, h2, x1, gate2, table)


def kernel(x, c, w_mod, b_mod, norm1_g, norm2_g, w_in, conv_qk, b_igate, b_fgate, mlstm_norm_g, qn_g, kn_g,
           cmp_pos_k, cmp_pos_v, cmp_k_w1, cmp_k_w2, cmp_v_w1, cmp_v_w2, w_out, peer_wq, peer_subkeys,
           peer_u, peer_v):
    b_, s_, d = x.shape
    n = b_ * s_
    for l in range(w_mod.shape[0]):
        mod = _modulation(c, w_mod[l], b_mod[l]).reshape(b_, 6, 1, d)
        shift1, scale1, gate1, shift2, scale2, gate2 = (mod[:, i] for i in range(6))
        x2 = x.reshape(n, d)
        z2 = _in_projection(x2, norm1_g[l], scale1, shift1, _pad_in_weights(w_in[l]), s_)
        z3 = z2.reshape(b_, s_, Z_COLS)
        gate_bias = jnp.concatenate([b_igate[l], b_fgate[l], jnp.zeros((LANES - 2 * M_HEADS,), F32)])
        hm = _mlstm(z3, conv_qk[l], gate_bias.reshape(1, LANES), mlstm_norm_g[l])
        qn, qa, kc, vc, ks, vs, kw, vw = _nsa_prep(z3, qn_g[l], kn_g[l])
        k_cmp = _compress(kc, cmp_pos_k[l], cmp_k_w1[l], cmp_k_w2[l], kn_g[l, 0], True)
        v_cmp = _compress(vc, cmp_pos_v[l], cmp_v_w1[l], cmp_v_w2[l], kn_g[l, 0], False)
        o_cmp, sel = _nsa_cmp(qn, k_cmp, v_cmp)
        o_win, o_sel = _nsa_attn(qn, qa, kw, vw, ks, vs, sel)
        x1, h2 = _out_projection(hm.reshape(n, M_WIDTH), o_cmp.reshape(n, A_WIDTH), o_sel.reshape(n, A_WIDTH),
                                 o_win.reshape(n, A_WIDTH), z2, x2, w_out[l], gate1, norm2_g[l], scale2, shift2, s_)
        eidx, gate = _peer_route(h2, peer_wq[l], peer_subkeys[l], s_)
        table = jnp.concatenate([peer_u[l], peer_v[l]], axis=1)[:, None, :]
        x = _peer_experts(eidx, gate, h2, x1, gate2, table, s_).reshape(b_, s_, d)
    return x
```
